```python
import jax, jax.numpy as jnp
from jax import lax
import numpy as np

D_MODEL = 1024
BATCH = 16
SEQ = 2048
DEPTH = 2

HEAD_DIM = 64
Q_BLOCK = 128
EPS = 1e-6
A_HEADS = 8
A_GROUPS = ((128, 1), (512, 4), (2048, 16))
A_WIDTH = A_HEADS * HEAD_DIM
B_HEADS = 8
B_WIDTH = B_HEADS * HEAD_DIM
B_KV_DIM = HEAD_DIM
IDX_HEADS = 8
IDX_DIM = 32
IDX_TOPK = 256
C_HEADS = 16
C_WIDTH = C_HEADS * HEAD_DIM

EVEN_WIDTH = A_WIDTH + B_WIDTH
EVEN_SPLITS = (A_WIDTH, A_WIDTH, A_WIDTH, A_WIDTH,
               B_WIDTH, B_KV_DIM, B_KV_DIM, B_WIDTH,
               IDX_HEADS * IDX_DIM, IDX_DIM, IDX_HEADS)
ODD_SPLITS = (C_WIDTH, C_WIDTH, C_WIDTH, C_WIDTH, C_HEADS)

kernel_name = "hybrid_dilated_dsa_fox_block"


def rms_norm(x, g):
    xf = x.astype(jnp.float32)
    y = xf * lax.rsqrt(jnp.mean(xf * xf, axis=-1, keepdims=True) + EPS)
    return (y * g.astype(jnp.float32)).astype(x.dtype)


def alibi_slopes(n):
    return jnp.asarray([2.0 ** (-8.0 * (i + 1) / n) for i in range(n)], jnp.float32)


def split_cols(h, sizes):
    cuts = np.cumsum(np.asarray(sizes))[:-1].tolist()
    return jnp.split(h, cuts, axis=-1)


def dilated_window_attention(q, k, v, slopes):
    B, T, H, Dh = q.shape
    scale = Dh ** -0.5
    outs, lses = [], []
    for window, dil in A_GROUPS:
        span = window // dil
        n = T // dil
        nb = -(-n // Q_BLOCK)
        n_pad = nb * Q_BLOCK

        def to_sub(a):
            a = a.reshape(B, n, dil, H, Dh)
            a = jnp.pad(a, ((0, 0), (0, n_pad - n), (0, 0), (0, 0), (0, 0)))
            return a.reshape(B, nb, Q_BLOCK, dil, H, Dh)

        qs, ks, vs = to_sub(q), to_sub(k), to_sub(v)
        kc = jnp.concatenate([jnp.concatenate([jnp.zeros_like(ks[:, :1]), ks[:, :-1]], axis=1), ks], axis=2)
        vc = jnp.concatenate([jnp.concatenate([jnp.zeros_like(vs[:, :1]), vs[:, :-1]], axis=1), vs], axis=2)
        qi = jnp.arange(Q_BLOCK)[:, None]
        kj = jnp.arange(2 * Q_BLOCK)[None, :]
        dsub = Q_BLOCK + qi - kj
        key_sub = (jnp.arange(nb)[:, None, None] - 1) * Q_BLOCK + kj[None]
        valid = (dsub >= 0)[None] & (dsub <= span)[None] & (key_sub >= 0)
        bias = -slopes[:, None, None] * (dsub * dil).astype(jnp.float32)[None]
        logits = jnp.einsum('bnqrhd,bnkrhd->bnrhqk', qs, kc).astype(jnp.float32) * scale
        logits = logits + bias[None, None, None]
        logits = jnp.where(valid[None, :, None, None], logits, -jnp.inf)
        m = jnp.max(logits, axis=-1, keepdims=True)
        p = jnp.exp(logits - m)
        l = jnp.sum(p, axis=-1, keepdims=True)
        o = jnp.einsum('bnrhqk,bnkrhd->bnqrhd', (p / l).astype(v.dtype), vc)
        lse = (m + jnp.log(l))[..., 0]
        o = o.reshape(B, n_pad, dil, H, Dh)[:, :n].reshape(B, T, H, Dh)
        lse = lse.transpose(0, 1, 4, 2, 3).reshape(B, n_pad, dil, H)[:, :n].reshape(B, T, H)
        outs.append(o)
        lses.append(lse)
    w = jax.nn.softmax(jnp.stack(lses, axis=0), axis=0)
    o = jnp.sum(w[..., None] * jnp.stack(outs, axis=0).astype(jnp.float32), axis=0)
    return o.astype(q.dtype)


def indexed_sparse_attention(q, k, v, q_idx, k_idx, w_idx, slopes, topk):
    B, T, H, Dh = q.shape
    scale = Dh ** -0.5
    nb = T // Q_BLOCK
    pos = jnp.arange(T)
    bidx = jnp.arange(B)[:, None, None]

    def block(i):
        start = i * Q_BLOCK
        tq = start + jnp.arange(Q_BLOCK)
        qb = lax.dynamic_slice_in_dim(q, start, Q_BLOCK, axis=1)
        qib = lax.dynamic_slice_in_dim(q_idx, start, Q_BLOCK, axis=1)
        wib = lax.dynamic_slice_in_dim(w_idx, start, Q_BLOCK, axis=1).astype(jnp.float32)
        s = jnp.einsum('bqhd,bsd->bqhs', qib, k_idx).astype(jnp.float32)
        score = jnp.einsum('bqh,bqhs->bqs', wib, jax.nn.relu(s))
        score = jnp.where(pos[None, None, :] <= tq[None, :, None], score, -jnp.inf)
        _, sel = lax.top_k(score, topk)
        ks = k[bidx, sel]
        vs = v[bidx, sel]
        dist = tq[None, :, None] - sel
        logits = jnp.einsum('bqhd,bqkd->bhqk', qb, ks).astype(jnp.float32) * scale
        logits = logits - slopes[None, :, None, None] * dist[:, None].astype(jnp.float32)
        logits = jnp.where((dist >= 0)[:, None], logits, -jnp.inf)
        p = jax.nn.softmax(logits, axis=-1)
        return jnp.einsum('bhqk,bqkd->bqhd', p.astype(v.dtype), vs)

    o = lax.map(block, jnp.arange(nb))
    return o.transpose(1, 0, 2, 3, 4).reshape(B, T, H, Dh)


def forgetting_attention(q, k, v, log_f):
    B, T, H, Dh = q.shape
    scale = Dh ** -0.5
    nb = T // Q_BLOCK
    pos = jnp.arange(T)
    cT = jnp.cumsum(log_f, axis=1).transpose(0, 2, 1)

    def block(i):
        start = i * Q_BLOCK
        tq = start + jnp.arange(Q_BLOCK)
        qb = lax.dynamic_slice_in_dim(q, start, Q_BLOCK, axis=1)
        cb = lax.dynamic_slice_in_dim(cT, start, Q_BLOCK, axis=2)
        logits = jnp.einsum('bqhd,bshd->bhqs', qb, k).astype(jnp.float32) * scale
        logits = logits + (cb[..., None] - cT[:, :, None, :])
        logits = jnp.where(pos[None, None, None, :] <= tq[None, None, :, None], logits, -jnp.inf)
        p = jax.nn.softmax(logits, axis=-1)
        return jnp.einsum('bhqs,bshd->bqhd', p.astype(v.dtype), v)

    o = lax.map(block, jnp.arange(nb))
    return o.transpose(1, 0, 2, 3, 4).reshape(B, T, H, Dh)


def even_layer(x, norm_g, w_in, q_norm_a, k_norm_a, q_norm_b, k_norm_b, k_norm_idx, w_out):
    B, T, _ = x.shape
    h = rms_norm(x, norm_g) @ w_in
    qa, ka, va, za, qb, kb, vb, zb, qi, ki, wi = split_cols(h, EVEN_SPLITS)
    qa = rms_norm(qa.reshape(B, T, A_HEADS, HEAD_DIM), q_norm_a)
    ka = rms_norm(ka.reshape(B, T, A_HEADS, HEAD_DIM), k_norm_a)
    va = va.reshape(B, T, A_HEADS, HEAD_DIM)
    ya = dilated_window_attention(qa, ka, va, alibi_slopes(A_HEADS))
    qb = rms_norm(qb.reshape(B, T, B_HEADS, HEAD_DIM), q_norm_b)
    kb = rms_norm(kb, k_norm_b)
    qi = qi.reshape(B, T, IDX_HEADS, IDX_DIM)
    ki = rms_norm(ki, k_norm_idx)
    wi = wi * (IDX_HEADS ** -0.5 * IDX_DIM ** -0.5)
    topk = min(IDX_TOPK, T // 4)
    yb = indexed_sparse_attention(qb, kb, vb, qi, ki, wi, alibi_slopes(B_HEADS), topk)
    y = jnp.concatenate([ya.reshape(B, T, A_WIDTH) * jax.nn.silu(za),
                         yb.reshape(B, T, B_WIDTH) * jax.nn.silu(zb)], axis=-1)
    return x + y @ w_out


def odd_layer(x, norm_g, w_in, b_forget, q_norm, k_norm, w_out):
    B, T, _ = x.shape
    h = rms_norm(x, norm_g) @ w_in
    q, k, v, z, fg = split_cols(h, ODD_SPLITS)
    q = rms_norm(q.reshape(B, T, C_HEADS, HEAD_DIM), q_norm)
    k = rms_norm(k.reshape(B, T, C_HEADS, HEAD_DIM), k_norm)
    v = v.reshape(B, T, C_HEADS, HEAD_DIM)
    log_f = jax.nn.log_sigmoid((fg + b_forget).astype(jnp.float32))
    y = forgetting_attention(q, k, v, log_f)
    return x + (y.reshape(B, T, C_WIDTH) * jax.nn.silu(z)) @ w_out


def setup_inputs(seed: int = 0) -> dict:
    key = jax.random.key(seed)
    ks = jax.random.split(key, 16)
    n_even = (DEPTH + 1) // 2
    n_odd = DEPTH // 2
    even_in = sum(EVEN_SPLITS)
    odd_in = sum(ODD_SPLITS)

    def gain(k, *shape):
        return 1.0 + 0.02 * jax.random.normal(k, shape, jnp.float32)

    def dense(k, *shape):
        return jax.random.normal(k, shape, jnp.float32) * (shape[-2] ** -0.5)

    return {
        "x": jax.random.normal(ks[0], (BATCH, SEQ, D_MODEL), jnp.float32),
        "even_norm": gain(ks[1], n_even, D_MODEL),
        "even_w_in": dense(ks[2], n_even, D_MODEL, even_in),
        "even_q_norm_a": gain(ks[3], n_even, HEAD_DIM),
        "even_k_norm_a": gain(ks[4], n_even, HEAD_DIM),
        "even_q_norm_b": gain(ks[5], n_even, HEAD_DIM),
        "even_k_norm_b": gain(ks[6], n_even, HEAD_DIM),
        "even_k_norm_idx": gain(ks[7], n_even, IDX_DIM),
        "even_w_out": dense(ks[8], n_even, EVEN_WIDTH, D_MODEL),
        "odd_norm": gain(ks[9], n_odd, D_MODEL),
        "odd_w_in": dense(ks[10], n_odd, D_MODEL, odd_in),
        "odd_b_forget": 3.0 + 0.5 * jax.random.normal(ks[11], (n_odd, C_HEADS), jnp.float32),
        "odd_q_norm": gain(ks[12], n_odd, HEAD_DIM),
        "odd_k_norm": gain(ks[13], n_odd, HEAD_DIM),
        "odd_w_out": dense(ks[14], n_odd, C_WIDTH, D_MODEL),
    }


def reference(x, even_norm, even_w_in, even_q_norm_a, even_k_norm_a, even_q_norm_b,
              even_k_norm_b, even_k_norm_idx, even_w_out, odd_norm, odd_w_in,
              odd_b_forget, odd_q_norm, odd_k_norm, odd_w_out):
    for layer in range(DEPTH):
        j = layer // 2
        if layer % 2 == 0:
            x = even_layer(x, even_norm[j], even_w_in[j], even_q_norm_a[j], even_k_norm_a[j],
                           even_q_norm_b[j], even_k_norm_b[j], even_k_norm_idx[j], even_w_out[j])
        else:
            x = odd_layer(x, odd_norm[j], odd_w_in[j], odd_b_forget[j], odd_q_norm[j],
                          odd_k_norm[j], odd_w_out[j])
    return x
```

```python
import functools

import numpy as np
import jax
import jax.numpy as jnp
from jax import lax
from jax.experimental import pallas as pl
from jax.experimental.pallas import tpu as pltpu

F32 = jnp.float32
BF16 = jnp.bfloat16

D_MODEL = 1024
BATCH = 16
SEQ = 2048
HEAD_DIM = 64
Q_BLOCK = 128
EPS = 1e-6
LANES = 128

A_HEADS = 8
A_GROUPS = ((128, 1), (512, 4), (2048, 16))
A_WIDTH = A_HEADS * HEAD_DIM
B_HEADS = 8
B_WIDTH = B_HEADS * HEAD_DIM
IDX_HEADS = 8
IDX_DIM = 32
IDX_TOPK = 256
C_HEADS = 16
C_WIDTH = C_HEADS * HEAD_DIM

E_QA, E_KA, E_VA, E_ZA = 0, 512, 1024, 1536
E_QB, E_ZB, E_QI, E_KV, E_KIW = 2048, 2560, 3072, 3328, 3456
E_COLS = 3584
O_Q, O_K, O_V, O_Z, O_FG = 0, 1024, 2048, 3072, 4096
O_COLS = 4224

ROW_TILE = 256
VMEM_LIMIT = 48 * 1024 * 1024
NEG_INF = float("-inf")
F32_LOWEST = float(np.finfo(np.float32).min)
INT_MIN = -(2 ** 31)

_NT = (((1,), (1,)), ((), ()))


def _params(*sem):
    return pltpu.CompilerParams(dimension_semantics=sem, vmem_limit_bytes=VMEM_LIMIT)


def _dot(a, b):
    return jnp.dot(a, b, preferred_element_type=F32)


def _dot_nt(a, b):
    return lax.dot_general(a, b, _NT, preferred_element_type=F32)


def _lane_iota(shape):
    return lax.broadcasted_iota(jnp.int32, shape, len(shape) - 1)


def _segment_mean_sq(x, seg):
    r = lax.broadcasted_iota(jnp.int32, (LANES, LANES), 0) // seg
    c = lax.broadcasted_iota(jnp.int32, (LANES, LANES), 1) // seg
    bd = jnp.where(r == c, 1.0 / seg, 0.0).astype(BF16)
    xx = x * x
    hi = xx.astype(BF16)
    lo = (xx - hi.astype(F32)).astype(BF16)
    return _dot(hi, bd) + _dot(lo, bd)


def _head_rms_norm(x, g):
    w = x.shape[-1]
    tiles = []
    for c in range(0, w, LANES):
        xt = x[:, c:c + LANES]
        ms = _segment_mean_sq(xt, HEAD_DIM)
        tiles.append(xt * lax.rsqrt(ms + EPS) * g[:, c:c + LANES])
    return tiles[0] if len(tiles) == 1 else jnp.concatenate(tiles, axis=-1)


def _silu(z):
    return z / (1.0 + jnp.exp(-z))


def _inproj_kernel(x_ref, g_ref, w_ref, o_ref, *, chunk):
    x = x_ref[...]
    ms = jnp.mean(x * x, axis=-1, keepdims=True)
    xn = (x * lax.rsqrt(ms + EPS) * g_ref[...]).astype(BF16)
    for c in range(0, o_ref.shape[-1], chunk):
        o_ref[:, c:c + chunk] = _dot(xn, w_ref[:, c:c + chunk])


def _inproj(x2d, g, w_bf16, chunk):
    m, d = x2d.shape
    n = w_bf16.shape[1]
    return pl.pallas_call(
        functools.partial(_inproj_kernel, chunk=chunk),
        grid=(m // ROW_TILE,),
        in_specs=[
            pl.BlockSpec((ROW_TILE, d), lambda i: (i, 0)),
            pl.BlockSpec((1, d), lambda i: (0, 0)),
            pl.BlockSpec((d, n), lambda i: (0, 0)),
        ],
        out_specs=pl.BlockSpec((ROW_TILE, n), lambda i: (i, 0)),
        out_shape=jax.ShapeDtypeStruct((m, n), F32),
        compiler_params=_params("arbitrary"),
        name="inproj",
    )(x2d, g, w_bf16)


def _dilated_kernel(q_ref, k_ref, v_ref, gq_ref, gk_ref, sl_ref, o_ref, l_ref,
                    qn_s, kn_s, vb_s, *, n, cw, dil):
    nb = n // Q_BLOCK
    qn_s[...] = (_head_rms_norm(q_ref[...], gq_ref[...]) * (HEAD_DIM ** -0.5)).astype(BF16)
    kn_s[...] = _head_rms_norm(k_ref[...], gk_ref[...]).astype(BF16)
    vb_s[...] = v_ref[...].astype(BF16)

    lane = _lane_iota((1, LANES))
    low = lane < HEAD_DIM
    qi = lax.broadcasted_iota(jnp.int32, (Q_BLOCK, 2 * Q_BLOCK), 0)
    kj = lax.broadcasted_iota(jnp.int32, (Q_BLOCK, 2 * Q_BLOCK), 1)
    dsub = Q_BLOCK + qi - kj
    span = Q_BLOCK
    base = jnp.where(dsub >= 0, jnp.where(dsub <= span, (-dil * dsub).astype(F32), NEG_INF), NEG_INF)

    for p in range(cw // LANES):
        sl = slice(p * LANES, (p + 1) * LANES)
        slopes = sl_ref[:, sl]
        biases = (base * slopes[:, 0:1], base * slopes[:, HEAD_DIM:HEAD_DIM + 1])

        def block(r0, first, sl=sl, biases=biases):
            qb = qn_s[pl.ds(r0, Q_BLOCK), sl].astype(F32)
            if first:
                kk = kn_s[0:Q_BLOCK, sl]
                vv = vb_s[0:Q_BLOCK, sl]
            else:
                kk = kn_s[pl.ds(r0 - Q_BLOCK, 2 * Q_BLOCK), sl]
                vv = vb_s[pl.ds(r0 - Q_BLOCK, 2 * Q_BLOCK), sl]
            outs, lses = [], []
            for e in range(2):
                qm = jnp.where(low if e == 0 else ~low, qb, 0.0).astype(BF16)
                s = _dot_nt(qm, kk)
                bias = biases[e][:, Q_BLOCK:] if first else biases[e]
                lg = s + bias
                m = jnp.max(lg, axis=-1, keepdims=True)
                pr = jnp.exp(lg - m)
                l = jnp.sum(pr, axis=-1, keepdims=True)
                outs.append(_dot(pr.astype(BF16), vv) / l)
                lses.append(m + jnp.log(l))
            o_ref[pl.ds(r0, Q_BLOCK), sl] = jnp.where(low, outs[0], outs[1])
            l_ref[pl.ds(r0, Q_BLOCK), sl] = jnp.where(low, lses[0], lses[1])

        block(0, True)
        if nb > 1:
            if nb <= 4:
                for j in range(1, nb):
                    block(j * Q_BLOCK, False)
            else:
                def body(j, carry):
                    block(pl.multiple_of(j * Q_BLOCK, Q_BLOCK), False)
                    return carry
                lax.fori_loop(1, nb, body, 0)


def _dilated_group(h0, gq_tile, gk_tile, slope_tile, dil, cw):
    n = SEQ // dil
    hr = h0.reshape(BATCH, n, dil * E_COLS)
    per_res = E_COLS // cw
    k_off, v_off = E_KA // cw, E_VA // cw
    out_res = A_WIDTH // cw
    blk = lambda off: pl.BlockSpec((None, n, cw), lambda b, r, c: (b, 0, r * per_res + off + c))
    par = pl.BlockSpec((1, cw), lambda b, r, c: (0, c))
    out_spec = pl.BlockSpec((None, n, cw), lambda b, r, c: (b, 0, r * out_res + c))
    out_sds = jax.ShapeDtypeStruct((BATCH, n, dil * A_WIDTH), F32)
    o, l = pl.pallas_call(
        functools.partial(_dilated_kernel, n=n, cw=cw, dil=dil),
        grid=(BATCH, dil, A_WIDTH // cw),
        in_specs=[blk(0), blk(k_off), blk(v_off), par, par, par],
        out_specs=[out_spec, out_spec],
        out_shape=[out_sds, out_sds],
        scratch_shapes=[pltpu.VMEM((n, cw), BF16)] * 3,
        compiler_params=_params("arbitrary", "arbitrary", "arbitrary"),
        name=f"dilated_d{dil}",
    )(hr, hr, hr, gq_tile, gk_tile, slope_tile)
    return o.reshape(BATCH * SEQ, A_WIDTH), l.reshape(BATCH * SEQ, A_WIDTH)


def _count_ge(score_ref, s_len, p):
    x = score_ref[:, 0:s_len]
    return jnp.sum(jnp.where(x >= p, 1.0, 0.0), axis=-1, keepdims=True)


def _key_to_float(key):
    bits = key ^ ((key >> 31) & 0x7FFFFFFF)
    return lax.bitcast_convert_type(bits, F32)


def _dsa_kernel(qb_ref, kv_ref, qi_ref, kiw_ref, wq_ref, gq_ref, gk_ref, gi_ref, o_ref,
                kip_s, kp_s, vp_s, score_s, bias_s, *, buckets):
    i = pl.program_id(1)
    lane = _lane_iota((1, LANES))
    kf = float(IDX_TOPK)

    @pl.when(i == 0)
    def _prep():
        kiw = kiw_ref[...]
        ki = jnp.where(lane < IDX_DIM, kiw, 0.0)
        ms = jnp.sum(ki * ki, axis=-1, keepdims=True) * (1.0 / IDX_DIM)
        kin = ki * lax.rsqrt(ms + EPS) * gi_ref[...]
        kip_s[0] = kin.astype(BF16)
        for c in range(1, LANES // IDX_DIM):
            kip_s[c] = pltpu.roll(kin, IDX_DIM * c, axis=1).astype(BF16)
        kv = kv_ref[...]
        kb = jnp.where(lane < HEAD_DIM, kv, 0.0)
        ms = jnp.sum(kb * kb, axis=-1, keepdims=True) * (1.0 / HEAD_DIM)
        kbn = kb * lax.rsqrt(ms + EPS) * gk_ref[...]
        kp_s[0] = kbn.astype(BF16)
        kp_s[1] = pltpu.roll(kbn, HEAD_DIM, axis=1).astype(BF16)
        vhi = jnp.where(lane >= HEAD_DIM, kv, 0.0)
        vp_s[1] = vhi.astype(BF16)
        vp_s[0] = pltpu.roll(vhi, HEAD_DIM, axis=1).astype(BF16)

    qn = (_head_rms_norm(qb_ref[...], gq_ref[...]) * (HEAD_DIM ** -0.5)).astype(BF16)
    qidx = qi_ref[...].astype(BF16)
    wq = wq_ref[...] * (IDX_HEADS ** -0.5 * IDX_DIM ** -0.5)
    t_idx = i * Q_BLOCK + lax.broadcasted_iota(jnp.int32, (Q_BLOCK, 1), 0)
    n_causal = (t_idx + 1).astype(F32)
    short = n_causal <= kf

    def body(s_len):
        s_idx = lax.broadcasted_iota(jnp.int32, (1, s_len), 1)
        causal = s_idx <= t_idx

        score = jnp.zeros((Q_BLOCK, s_len), F32)
        for h in range(IDX_HEADS):
            t = h // 4
            sh = _dot_nt(qidx[:, t * LANES:(t + 1) * LANES], kip_s[h % 4, 0:s_len, :])
            score = score + wq[:, IDX_DIM + h:IDX_DIM + h + 1] * jnp.maximum(sh, 0.0)
        score_s[:, 0:s_len] = jnp.where(causal, score, NEG_INF)

        def step(it, tu):
            bit = lax.shift_left(jnp.int32(1), 31 - it)
            cand = tu | bit
            p = _key_to_float(cand ^ INT_MIN)
            cnt = _count_ge(score_s, s_len, p)
            return jnp.where(cnt >= kf, cand, tu)

        tu = lax.fori_loop(0, 32, step, jnp.zeros((Q_BLOCK, 1), jnp.int32))
        key = tu ^ INT_MIN
        lo = jnp.where(short, F32_LOWEST, _key_to_float(key))
        hi = _key_to_float(key + 1)
        cnt_lo = jnp.where(short, kf, _count_ge(score_s, s_len, lo))
        bias_s[:, 0:s_len] = jnp.where(score_s[:, 0:s_len] >= lo, 0.0, NEG_INF)

        unresolved = jnp.max(jnp.where(cnt_lo != kf, 1.0, 0.0)) > 0.0

        @pl.when(unresolved)
        def _ties():
            def active(lo_, hi_, c_):
                mid = lo_ + (hi_ - lo_) * 0.5
                act = jnp.where(c_ != kf, jnp.where(mid > lo_, jnp.where(mid < hi_, 1.0, 0.0), 0.0), 0.0)
                return mid, act

            def cond(c):
                return c[4] > 0

            def wbody(c):
                lo_, hi_, c_, it, _ = c
                mid, act = active(lo_, hi_, c_)
                cnt = _count_ge(score_s, s_len, mid)
                up = jnp.where(cnt >= kf, act, 0.0) > 0.0
                dn = jnp.where(cnt >= kf, 0.0, act) > 0.0
                lo2 = jnp.where(up, mid, lo_)
                c2 = jnp.where(up, cnt, c_)
                hi2 = jnp.where(dn, mid, hi_)
                _, act2 = active(lo2, hi2, c2)
                go = jnp.where(jnp.max(act2) > 0.0, 1, 0) * jnp.where(it < 200, 1, 0)
                return lo2, hi2, c2, it + 1, go.astype(jnp.int32)

            _, act0 = active(lo, hi, cnt_lo)
            go0 = jnp.where(jnp.max(act0) > 0.0, 1, 0).astype(jnp.int32)
            lo_f, _, _, _, _ = lax.while_loop(cond, wbody, (lo, hi, cnt_lo, jnp.int32(0), go0))

            sc = score_s[:, 0:s_len]
            gt = sc > lo_f
            need = kf - jnp.sum(jnp.where(gt, 1.0, 0.0), axis=-1, keepdims=True)
            eq = jnp.where(sc == lo_f, 1.0, 0.0)
            rr = lax.broadcasted_iota(jnp.int32, (LANES, LANES), 0)
            cc = lax.broadcasted_iota(jnp.int32, (LANES, LANES), 1)
            upper = jnp.where(rr < cc, 1.0, 0.0).astype(BF16)
            carry = jnp.zeros((Q_BLOCK, 1), F32)
            for c0 in range(0, s_len, LANES):
                eq_c = eq[:, c0:c0 + LANES]
                prefix = _dot(eq_c.astype(BF16), upper) + carry
                keep = jnp.where(gt[:, c0:c0 + LANES], 1.0, jnp.where(prefix < need, eq_c, 0.0))
                bias_s[:, c0:c0 + LANES] = jnp.where(keep > 0.0, 0.0, NEG_INF)
                carry = carry + jnp.sum(eq_c, axis=-1, keepdims=True)

        dist = (t_idx - s_idx).astype(F32)
        for pr_i in range(B_HEADS // 2):
            acc = jnp.zeros((Q_BLOCK, LANES), F32)
            qp = qn[:, pr_i * LANES:(pr_i + 1) * LANES]
            for e in range(2):
                h = 2 * pr_i + e
                slope = 2.0 ** (-8.0 * (h + 1) / B_HEADS)
                s = _dot_nt(qp, kp_s[e, 0:s_len, :])
                lg = (s - slope * dist) + bias_s[:, 0:s_len]
                m = jnp.max(lg, axis=-1, keepdims=True)
                pexp = jnp.exp(lg - m)
                l = jnp.sum(pexp, axis=-1, keepdims=True)
                acc = acc + _dot(pexp.astype(BF16), vp_s[e, 0:s_len, :]) / l
            o_ref[:, pr_i * LANES:(pr_i + 1) * LANES] = acc

    per = (SEQ // Q_BLOCK) // len(buckets)
    for bi, s_len in enumerate(buckets):
        pl.when(i // per == bi)(functools.partial(body, s_len))


def _dsa(h0, gq_tile, gk_pad, gi_pad, buckets):
    h3 = h0.reshape(BATCH, SEQ, E_COLS)
    nq = SEQ // Q_BLOCK
    return pl.pallas_call(
        functools.partial(_dsa_kernel, buckets=buckets),
        grid=(BATCH, nq),
        in_specs=[
            pl.BlockSpec((None, Q_BLOCK, B_WIDTH), lambda b, i: (b, i, E_QB // B_WIDTH)),
            pl.BlockSpec((None, SEQ, LANES), lambda b, i: (b, 0, E_KV // LANES)),
            pl.BlockSpec((None, Q_BLOCK, 2 * LANES), lambda b, i: (b, i, E_QI // (2 * LANES))),
            pl.BlockSpec((None, SEQ, LANES), lambda b, i: (b, 0, E_KIW // LANES)),
            pl.BlockSpec((None, Q_BLOCK, LANES), lambda b, i: (b, i, E_KIW // LANES)),
            pl.BlockSpec((1, B_WIDTH), lambda b, i: (0, 0)),
            pl.BlockSpec((1, LANES), lambda b, i: (0, 0)),
            pl.BlockSpec((1, LANES), lambda b, i: (0, 0)),
        ],
        out_specs=pl.BlockSpec((None, Q_BLOCK, B_WIDTH), lambda b, i: (b, i, 0)),
        out_shape=jax.ShapeDtypeStruct((BATCH, SEQ, B_WIDTH), F32),
        scratch_shapes=[
            pltpu.VMEM((LANES // IDX_DIM, SEQ, LANES), BF16),
            pltpu.VMEM((2, SEQ, LANES), BF16),
            pltpu.VMEM((2, SEQ, LANES), BF16),
            pltpu.VMEM((Q_BLOCK, SEQ), F32),
            pltpu.VMEM((Q_BLOCK, SEQ), F32),
        ],
        compiler_params=_params("arbitrary", "arbitrary"),
        name="dsa",
    )(h3, h3, h3, h3, h3, gq_tile, gk_pad, gi_pad).reshape(BATCH * SEQ, B_WIDTH)


def _outproj_even_kernel(o1, o2, o3, l1, l2, l3, za, yb, zb, x, w, out):
    la, lb, lc = l1[...], l2[...], l3[...]
    m = jnp.maximum(jnp.maximum(la, lb), lc)
    ea, eb, ec = jnp.exp(la - m), jnp.exp(lb - m), jnp.exp(lc - m)
    ya = (ea * o1[...] + eb * o2[...] + ec * o3[...]) / (ea + eb + ec)
    ga = (ya * _silu(za[...])).astype(BF16)
    gb = (yb[...] * _silu(zb[...])).astype(BF16)
    out[...] = x[...] + _dot(ga, w[0:A_WIDTH, :]) + _dot(gb, w[A_WIDTH:, :])


def _outproj_even(groups, h0, yb, x2d, w_bf16):
    m = x2d.shape[0]
    half = pl.BlockSpec((ROW_TILE, A_WIDTH), lambda i: (i, 0))
    (o1, l1), (o2, l2), (o3, l3) = groups
    return pl.pallas_call(
        _outproj_even_kernel,
        grid=(m // ROW_TILE,),
        in_specs=[half] * 6 + [
            pl.BlockSpec((ROW_TILE, A_WIDTH), lambda i: (i, E_ZA // A_WIDTH)),
            half,
            pl.BlockSpec((ROW_TILE, B_WIDTH), lambda i: (i, E_ZB // B_WIDTH)),
            pl.BlockSpec((ROW_TILE, D_MODEL), lambda i: (i, 0)),
            pl.BlockSpec((A_WIDTH + B_WIDTH, D_MODEL), lambda i: (0, 0)),
        ],
        out_specs=pl.BlockSpec((ROW_TILE, D_MODEL), lambda i: (i, 0)),
        out_shape=jax.ShapeDtypeStruct((m, D_MODEL), F32),
        compiler_params=_params("arbitrary"),
        name="outproj_even",
    )(o1, o2, o3, l1, l2, l3, h0, yb, h0, x2d, w_bf16)


def _forget_kernel(fg_ref, b_ref, ccol_ref, crow_ref):
    x = fg_ref[...] + b_ref[...]
    logf = jnp.minimum(x, 0.0) - jnp.log1p(jnp.exp(-jnp.abs(x)))
    rr = lax.broadcasted_iota(jnp.int32, (LANES, LANES), 0)
    cc = lax.broadcasted_iota(jnp.int32, (LANES, LANES), 1)
    tri = jnp.where(cc <= rr, 1.0, 0.0).astype(BF16)
    carry = jnp.zeros((1, LANES), F32)
    for blk in range(SEQ // LANES):
        rows = slice(blk * LANES, (blk + 1) * LANES)
        v = logf[rows, :]
        t0 = v.astype(BF16)
        r1 = v - t0.astype(F32)
        t1 = r1.astype(BF16)
        t2 = (r1 - t1.astype(F32)).astype(BF16)
        c = _dot(tri, t0) + _dot(tri, t1) + _dot(tri, t2) + carry
        ccol_ref[rows, :] = c
        crow_ref[:, rows] = c.T[0:C_HEADS, :]
        carry = c[LANES - 1:LANES, :]


def _forget_cumsum(h1, b_pad):
    h3 = h1.reshape(BATCH, SEQ, O_COLS)
    return pl.pallas_call(
        _forget_kernel,
        grid=(BATCH,),
        in_specs=[
            pl.BlockSpec((None, SEQ, LANES), lambda b: (b, 0, O_FG // LANES)),
            pl.BlockSpec((1, LANES), lambda b: (0, 0)),
        ],
        out_specs=[
            pl.BlockSpec((None, SEQ, LANES), lambda b: (b, 0, 0)),
            pl.BlockSpec((None, C_HEADS, SEQ), lambda b: (b, 0, 0)),
        ],
        out_shape=[
            jax.ShapeDtypeStruct((BATCH, SEQ, LANES), F32),
            jax.ShapeDtypeStruct((BATCH, C_HEADS, SEQ), F32),
        ],
        compiler_params=_params("arbitrary"),
        name="forget_cumsum",
    )(h3, b_pad)


def _fox_kernel(q_ref, k_ref, v_ref, ccol_ref, crow_ref, gq_ref, gk_ref, y_ref, qn_s, kp_s, vp_s):
    hp = pl.program_id(1)
    lane = _lane_iota((1, LANES))
    low = lane < HEAD_DIM
    qn_s[...] = (_head_rms_norm(q_ref[...], gq_ref[...]) * (HEAD_DIM ** -0.5)).astype(BF16)
    kn = _head_rms_norm(k_ref[...], gk_ref[...])
    v = v_ref[...]
    kp_s[0] = jnp.where(low, kn, 0.0).astype(BF16)
    kp_s[1] = jnp.where(low, 0.0, kn).astype(BF16)
    vp_s[0] = jnp.where(low, v, 0.0).astype(BF16)
    vp_s[1] = jnp.where(low, 0.0, v).astype(BF16)

    for iq in range(SEQ // Q_BLOCK):
        rows = slice(iq * Q_BLOCK, (iq + 1) * Q_BLOCK)
        s_len = (iq + 1) * Q_BLOCK
        qblk = qn_s[rows, :]
        ccol = ccol_ref[rows, :]
        t_idx = iq * Q_BLOCK + lax.broadcasted_iota(jnp.int32, (Q_BLOCK, 1), 0)
        s_idx = lax.broadcasted_iota(jnp.int32, (1, s_len), 1)
        causal = s_idx <= t_idx
        acc = jnp.zeros((Q_BLOCK, LANES), F32)
        for e in range(2):
            h = 2 * hp + e
            cq = jnp.sum(jnp.where(lane == h, ccol, 0.0), axis=-1, keepdims=True)
            cs = crow_ref[e:e + 1, 0:s_len]
            s = _dot_nt(qblk, kp_s[e, 0:s_len, :])
            lg = jnp.where(causal, s + (cq - cs), NEG_INF)
            m = jnp.max(lg, axis=-1, keepdims=True)
            pexp = jnp.exp(lg - m)
            l = jnp.sum(pexp, axis=-1, keepdims=True)
            acc = acc + _dot(pexp.astype(BF16), vp_s[e, 0:s_len, :]) / l
        y_ref[rows, :] = acc


def _fox(h1, ccol, crow, gq_tile, gk_tile):
    h3 = h1.reshape(BATCH, SEQ, O_COLS)
    crow4 = crow.reshape(BATCH, C_HEADS // 2, 2, SEQ)
    blk = lambda off: pl.BlockSpec((None, SEQ, LANES), lambda b, p: (b, 0, off // LANES + p))
    par = pl.BlockSpec((1, LANES), lambda b, p: (0, 0))
    return pl.pallas_call(
        _fox_kernel,
        grid=(BATCH, C_HEADS // 2),
        in_specs=[
            blk(O_Q), blk(O_K), blk(O_V),
            pl.BlockSpec((None, SEQ, LANES), lambda b, p: (b, 0, 0)),
            pl.BlockSpec((None, None, 2, SEQ), lambda b, p: (b, p, 0, 0)),
            par, par,
        ],
        out_specs=pl.BlockSpec((None, SEQ, LANES), lambda b, p: (b, 0, p)),
        out_shape=jax.ShapeDtypeStruct((BATCH, SEQ, C_WIDTH), F32),
        scratch_shapes=[
            pltpu.VMEM((SEQ, LANES), BF16),
            pltpu.VMEM((2, SEQ, LANES), BF16),
            pltpu.VMEM((2, SEQ, LANES), BF16),
        ],
        compiler_params=_params("arbitrary", "arbitrary"),
        name="fox",
    )(h3, h3, h3, ccol, crow4, gq_tile, gk_tile).reshape(BATCH * SEQ, C_WIDTH)


def _outproj_odd_kernel(y, z, x, w, out):
    g = (y[...] * _silu(z[...])).astype(BF16)
    out[...] = x[...] + _dot(g, w[...])


def _outproj_odd(y, h1, x2d, w_bf16):
    m = x2d.shape[0]
    full = pl.BlockSpec((ROW_TILE, D_MODEL), lambda i: (i, 0))
    return pl.pallas_call(
        _outproj_odd_kernel,
        grid=(m // ROW_TILE,),
        in_specs=[
            full,
            pl.BlockSpec((ROW_TILE, C_WIDTH), lambda i: (i, O_Z // C_WIDTH)),
            full,
            pl.BlockSpec((C_WIDTH, D_MODEL), lambda i: (0, 0)),
        ],
        out_specs=full,
        out_shape=jax.ShapeDtypeStruct((m, D_MODEL), F32),
        compiler_params=_params("arbitrary"),
        name="outproj_odd",
    )(y, h1, x2d, w_bf16)


def _even_weight(w):
    qa, ka, va, za, qb, kb, vb, zb, qi, ki, wi = jnp.split(
        w, np.cumsum([512, 512, 512, 512, 512, 64, 64, 512, 256, 32, 8])[:-1].tolist(), axis=-1)
    pad = jnp.zeros((w.shape[0], E_COLS - (E_KIW + IDX_DIM + IDX_HEADS)), w.dtype)
    return jnp.concatenate([qa, ka, va, za, qb, zb, qi, kb, vb, ki, wi, pad], axis=-1).astype(BF16)


def _pad_lanes(v, width):
    return jnp.pad(v, (0, width - v.shape[0])).reshape(1, width)


def _tile_heads(g, heads):
    return jnp.tile(g, heads).reshape(1, heads * HEAD_DIM)


def kernel(x, even_norm, even_w_in, even_q_norm_a, even_k_norm_a, even_q_norm_b, even_k_norm_b,
           even_k_norm_idx, even_w_out, odd_norm, odd_w_in, odd_b_forget, odd_q_norm, odd_k_norm,
           odd_w_out):
    assert x.shape == (BATCH, SEQ, D_MODEL)
    x2d = x.reshape(BATCH * SEQ, D_MODEL)

    h0 = _inproj(x2d, even_norm[0].reshape(1, D_MODEL), _even_weight(even_w_in[0]), chunk=512)
    slope_tile = jnp.asarray(
        np.repeat([2.0 ** (-8.0 * (i + 1) / A_HEADS) for i in range(A_HEADS)], HEAD_DIM), F32
    ).reshape(1, A_WIDTH)
    gqa = _tile_heads(even_q_norm_a[0], A_HEADS)
    gka = _tile_heads(even_k_norm_a[0], A_HEADS)
    groups = [
        _dilated_group(h0, gqa, gka, slope_tile, dil, cw)
        for (_, dil), cw in zip(A_GROUPS, (128, 128, 512))
    ]
    yb = _dsa(
        h0,
        _tile_heads(even_q_norm_b[0], B_HEADS),
        _pad_lanes(even_k_norm_b[0], LANES),
        _pad_lanes(even_k_norm_idx[0], LANES),
        buckets=(512, 1024, 1536, 2048),
    )
    x1 = _outproj_even(groups, h0, yb, x2d, even_w_out[0].astype(BF16))

    w1 = jnp.pad(odd_w_in[0], ((0, 0), (0, O_COLS - odd_w_in.shape[-1]))).astype(BF16)
    h1 = _inproj(x1, odd_norm[0].reshape(1, D_MODEL), w1, chunk=384)
    ccol, crow = _forget_cumsum(h1, _pad_lanes(odd_b_forget[0], LANES))
    y = _fox(h1, ccol, crow, _tile_heads(odd_q_norm[0], 2), _tile_heads(odd_k_norm[0], 2))
    out = _outproj_odd(y, h1, x1, odd_w_out[0].astype(BF16))
    return out.reshape(BATCH, SEQ, D_MODEL)
```

```python
import functools

import numpy as np
import jax
import jax.numpy as jnp
from jax import lax
from jax.experimental import pallas as pl
from jax.experimental.pallas import tpu as pltpu

F32 = jnp.float32
BF16 = jnp.bfloat16

D_MODEL = 1024
BATCH = 16
SEQ = 2048
HEAD_DIM = 64
Q_BLOCK = 128
QUARTER = SEQ // 4
EPS = 1e-6
LANES = 128

A_HEADS = 8
A_GROUPS = ((128, 1), (512, 4), (2048, 16))
A_WIDTH = A_HEADS * HEAD_DIM
B_HEADS = 8
B_WIDTH = B_HEADS * HEAD_DIM
IDX_HEADS = 8
IDX_DIM = 32
IDX_TOPK = 256
C_HEADS = 16
C_WIDTH = C_HEADS * HEAD_DIM

E_QA, E_KA, E_VA, E_ZA = 0, 512, 1024, 1536
E_QB, E_ZB, E_QI, E_KV, E_KIW = 2048, 2560, 3072, 3328, 3456
E_COLS = 3584
O_Q, O_K, O_V, O_Z, O_FG = 0, 1024, 2048, 3072, 4096
O_COLS = 4224

ROW_TILE = 256
VMEM_LIMIT = 48 * 1024 * 1024
NEG_INF = float("-inf")
F32_LOWEST = float(np.finfo(np.float32).min)
INT_MIN = -(2 ** 31)
LOG2E = float(np.log2(np.e))
FIRST_PER_ITER = 4
REST_PER_ITER = {1: 5, 4: 4}

_NT = (((1,), (1,)), ((), ()))


def _params(*sem):
    return pltpu.CompilerParams(dimension_semantics=sem, vmem_limit_bytes=VMEM_LIMIT)


def _dot(a, b):
    return jnp.dot(a, b, preferred_element_type=F32)


def _dot_nt(a, b):
    return lax.dot_general(a, b, _NT, preferred_element_type=F32)


def _lane_iota(shape):
    return lax.broadcasted_iota(jnp.int32, shape, len(shape) - 1)


def _segment_mean_sq(x, seg):
    r = lax.broadcasted_iota(jnp.int32, (LANES, LANES), 0) // seg
    c = lax.broadcasted_iota(jnp.int32, (LANES, LANES), 1) // seg
    bd = jnp.where(r == c, 1.0 / seg, 0.0).astype(BF16)
    xx = x * x
    hi = xx.astype(BF16)
    lo = (xx - hi.astype(F32)).astype(BF16)
    return _dot(hi, bd) + _dot(lo, bd)


def _head_rms_norm(x, g):
    w = x.shape[-1]
    tiles = []
    for c in range(0, w, LANES):
        xt = x[:, c:c + LANES]
        ms = _segment_mean_sq(xt, HEAD_DIM)
        tiles.append(xt * lax.rsqrt(ms + EPS) * g[:, c:c + LANES])
    return tiles[0] if len(tiles) == 1 else jnp.concatenate(tiles, axis=-1)


def _silu(z):
    return z / (1.0 + jnp.exp(-z))


def _inproj_kernel(x_ref, g_ref, w_ref, o_ref, *, chunk):
    x = x_ref[...]
    ms = jnp.mean(x * x, axis=-1, keepdims=True)
    xn = (x * lax.rsqrt(ms + EPS) * g_ref[...]).astype(BF16)
    for c in range(0, o_ref.shape[-1], chunk):
        o_ref[:, c:c + chunk] = _dot(xn, w_ref[:, c:c + chunk])


def _inproj(x2d, g, w_bf16, chunk):
    m, d = x2d.shape
    n = w_bf16.shape[1]
    return pl.pallas_call(
        functools.partial(_inproj_kernel, chunk=chunk),
        grid=(m // ROW_TILE,),
        in_specs=[
            pl.BlockSpec((ROW_TILE, d), lambda i: (i, 0)),
            pl.BlockSpec((1, d), lambda i: (0, 0)),
            pl.BlockSpec((d, n), lambda i: (0, 0)),
        ],
        out_specs=pl.BlockSpec((ROW_TILE, n), lambda i: (i, 0)),
        out_shape=jax.ShapeDtypeStruct((m, n), F32),
        compiler_params=_params("arbitrary"),
        name="inproj",
    )(x2d, g, w_bf16)


def _rows(start, size, stride):
    return pl.ds(start, size) if stride == 1 else pl.ds(start, size, stride=stride)


def _dilated_kernel(q_ref, k_ref, v_ref, gq_ref, gk_ref, sl_ref, y_ref,
                    qn_s, kn_s, q4_s, k4_s, v4_s, pv_s, m_s, l_s):
    qn_s[...] = _head_rms_norm(q_ref[...], gq_ref[...]) * (HEAD_DIM ** -0.5 * LOG2E)
    kn_s[...] = _head_rms_norm(k_ref[...], gk_ref[...])
    for c in range(4):
        dst = slice(c * QUARTER, (c + 1) * QUARTER)
        src = _rows(c, QUARTER, 4)
        q4_s[dst, :] = qn_s[src, :]
        k4_s[dst, :] = kn_s[src, :]
        v4_s[dst, :] = v_ref[src, :]

    lane = _lane_iota((1, LANES))
    low = lane < HEAD_DIM
    qi = lax.broadcasted_iota(jnp.int32, (Q_BLOCK, 2 * Q_BLOCK), 0)
    kj = lax.broadcasted_iota(jnp.int32, (Q_BLOCK, 2 * Q_BLOCK), 1)
    dsub = Q_BLOCK + qi - kj
    slopes = sl_ref[...] * LOG2E
    slope_pair = (slopes[:, 0:1], slopes[:, HEAD_DIM:HEAD_DIM + 1])

    for g, (window, dil) in enumerate(A_GROUPS):
        span = window // dil
        nb = SEQ // dil // Q_BLOCK
        base = jnp.where(dsub >= 0, jnp.where(dsub <= span, (-dil * dsub).astype(F32), NEG_INF), NEG_INF)
        biases = (base * slope_pair[0], base * slope_pair[1])
        q_src, k_src, v_src = (qn_s, kn_s, v_ref) if g == 0 else (q4_s, k4_s, v4_s)
        stride = 4 if dil == 16 else 1

        def load(q0, first, stride=stride, q_src=q_src, k_src=k_src, v_src=v_src):
            qr = _rows(q0, Q_BLOCK, stride)
            kr = qr if first else _rows(q0 - stride * Q_BLOCK, 2 * Q_BLOCK, stride)
            return q_src[qr, :], k_src[kr, :].astype(BF16), v_src[kr, :].astype(BF16)

        def compute(loaded, first, biases=biases):
            n = len(loaded)
            lgs = [
                _dot_nt(jnp.where(low if e == 0 else ~low, qb, 0.0).astype(BF16), kk)
                + (biases[e][:, Q_BLOCK:] if first else biases[e])
                for qb, kk, _ in loaded for e in range(2)
            ]
            ms = [jnp.max(lg, axis=-1, keepdims=True) for lg in lgs]
            prs = [jnp.exp2(lg - m) for lg, m in zip(lgs, ms)]
            ls = [jnp.sum(pr, axis=-1, keepdims=True) for pr in prs]
            pvs = [_dot(pr.astype(BF16), loaded[i // 2][2]) for i, pr in enumerate(prs)]
            return [
                (jnp.where(low, pvs[2 * b], pvs[2 * b + 1]), jnp.where(low, ms[2 * b], ms[2 * b + 1]),
                 jnp.where(low, ls[2 * b], ls[2 * b + 1]))
                for b in range(n)
            ]

        def store(q0, res, g=g, stride=stride):
            qr = _rows(q0, Q_BLOCK, stride)
            pv_s[g, qr, :] = res[0]
            m_s[g, qr, :] = res[1]
            l_s[g, qr, :] = res[2]

        def run_blocks(start_of, count, per_iter, first, load=load, compute=compute, store=store):
            def body(t, carry):
                starts = [start_of(t * per_iter + u) for u in range(per_iter)]
                loaded = [load(q0, first) for q0 in starts]
                results = compute(loaded, first)
                for q0, res in zip(starts, results):
                    store(q0, res)
                return carry
            trips = count // per_iter
            if trips == 1:
                body(0, 0)
            else:
                lax.fori_loop(0, trips, body, 0)

        per = nb - 1
        if dil == 1:
            first_start = lambda idx: 0
            rest_start = lambda idx: Q_BLOCK * (1 + idx)
        elif dil == 4:
            first_start = lambda idx: idx * QUARTER
            rest_start = lambda idx, per=per: lax.div(idx, per) * QUARTER + Q_BLOCK * (1 + lax.rem(idx, per))
        else:
            first_start = lambda idx: lax.rem(idx, 4) * QUARTER + lax.div(idx, 4)
            rest_start = None
        run_blocks(first_start, dil, min(dil, FIRST_PER_ITER), True)
        if per:
            run_blocks(rest_start, dil * per, REST_PER_ITER[dil], False)

    chunk = 2 * Q_BLOCK
    for c in range(4):
        for o in range(0, QUARTER, chunk):
            nat = _rows(c + 4 * o, chunk, 4)
            grp = slice(c * QUARTER + o, c * QUARTER + o + chunk)
            m0, m1, m2 = m_s[0, nat, :], m_s[1, grp, :], m_s[2, grp, :]
            mm = jnp.maximum(jnp.maximum(m0, m1), m2)
            w0, w1, w2 = jnp.exp2(m0 - mm), jnp.exp2(m1 - mm), jnp.exp2(m2 - mm)
            num = w0 * pv_s[0, nat, :] + w1 * pv_s[1, grp, :] + w2 * pv_s[2, grp, :]
            den = w0 * l_s[0, nat, :] + w1 * l_s[1, grp, :] + w2 * l_s[2, grp, :]
            y_ref[nat, :] = num / den


def _dilated(h0, gq_tile, gk_tile, slope_tile):
    h3 = h0.reshape(BATCH, SEQ, E_COLS)
    blk = lambda off: pl.BlockSpec((None, SEQ, LANES), lambda b, p: (b, 0, off // LANES + p))
    par = pl.BlockSpec((1, LANES), lambda b, p: (0, p))
    seq_tile = pltpu.VMEM((SEQ, LANES), F32)
    grp_tile = pltpu.VMEM((len(A_GROUPS), SEQ, LANES), F32)
    return pl.pallas_call(
        _dilated_kernel,
        grid=(BATCH, A_WIDTH // LANES),
        in_specs=[blk(E_QA), blk(E_KA), blk(E_VA), par, par, par],
        out_specs=pl.BlockSpec((None, SEQ, LANES), lambda b, p: (b, 0, p)),
        out_shape=jax.ShapeDtypeStruct((BATCH, SEQ, A_WIDTH), F32),
        scratch_shapes=[seq_tile] * 5 + [grp_tile] * 3,
        compiler_params=_params("arbitrary", "arbitrary"),
        name="dilated",
    )(h3, h3, h3, gq_tile, gk_tile, slope_tile).reshape(BATCH * SEQ, A_WIDTH)


def _count_ge(score_ref, s_len, p):
    x = score_ref[:, 0:s_len]
    return jnp.sum(jnp.where(x >= p, 1.0, 0.0), axis=-1, keepdims=True)


def _key_to_float(key):
    bits = key ^ ((key >> 31) & 0x7FFFFFFF)
    return lax.bitcast_convert_type(bits, F32)


def _dsa_kernel(qb_ref, kv_ref, qi_ref, kiw_ref, wq_ref, gq_ref, gk_ref, gi_ref, o_ref,
                kip_s, kp_s, vp_s, score_s, bias_s, *, buckets):
    i = pl.program_id(1)
    lane = _lane_iota((1, LANES))
    kf = float(IDX_TOPK)

    @pl.when(i == 0)
    def _prep():
        kiw = kiw_ref[...]
        ki = jnp.where(lane < IDX_DIM, kiw, 0.0)
        ms = jnp.sum(ki * ki, axis=-1, keepdims=True) * (1.0 / IDX_DIM)
        kin = ki * lax.rsqrt(ms + EPS) * gi_ref[...]
        kip_s[0] = kin.astype(BF16)
        for c in range(1, LANES // IDX_DIM):
            kip_s[c] = pltpu.roll(kin, IDX_DIM * c, axis=1).astype(BF16)
        kv = kv_ref[...]
        kb = jnp.where(lane < HEAD_DIM, kv, 0.0)
        ms = jnp.sum(kb * kb, axis=-1, keepdims=True) * (1.0 / HEAD_DIM)
        kbn = kb * lax.rsqrt(ms + EPS) * gk_ref[...]
        kp_s[0] = kbn.astype(BF16)
        kp_s[1] = pltpu.roll(kbn, HEAD_DIM, axis=1).astype(BF16)
        vhi = jnp.where(lane >= HEAD_DIM, kv, 0.0)
        vp_s[1] = vhi.astype(BF16)
        vp_s[0] = pltpu.roll(vhi, HEAD_DIM, axis=1).astype(BF16)

    qn = (_head_rms_norm(qb_ref[...], gq_ref[...]) * (HEAD_DIM ** -0.5 * LOG2E)).astype(BF16)
    qidx = qi_ref[...].astype(BF16)
    wq = wq_ref[...] * (IDX_HEADS ** -0.5 * IDX_DIM ** -0.5)
    t_idx = i * Q_BLOCK + lax.broadcasted_iota(jnp.int32, (Q_BLOCK, 1), 0)
    n_causal = (t_idx + 1).astype(F32)
    short = n_causal <= kf

    def body(s_len):
        s_idx = lax.broadcasted_iota(jnp.int32, (1, s_len), 1)
        causal = s_idx <= t_idx

        score = jnp.zeros((Q_BLOCK, s_len), F32)
        for h in range(IDX_HEADS):
            t = h // 4
            sh = _dot_nt(qidx[:, t * LANES:(t + 1) * LANES], kip_s[h % 4, 0:s_len, :])
            score = score + wq[:, IDX_DIM + h:IDX_DIM + h + 1] * jnp.maximum(sh, 0.0)
        score_s[:, 0:s_len] = jnp.where(causal, score, NEG_INF)

        def step(it, tu):
            bit = lax.shift_left(jnp.int32(1), 31 - it)
            cand = tu | bit
            p = _key_to_float(cand ^ INT_MIN)
            cnt = _count_ge(score_s, s_len, p)
            return jnp.where(cnt >= kf, cand, tu)

        tu = lax.fori_loop(0, 32, step, jnp.zeros((Q_BLOCK, 1), jnp.int32))
        key = tu ^ INT_MIN
        lo = jnp.where(short, F32_LOWEST, _key_to_float(key))
        hi = _key_to_float(key + 1)
        cnt_lo = jnp.where(short, kf, _count_ge(score_s, s_len, lo))
        bias_s[:, 0:s_len] = jnp.where(score_s[:, 0:s_len] >= lo, 0.0, NEG_INF)

        unresolved = jnp.max(jnp.where(cnt_lo != kf, 1.0, 0.0)) > 0.0

        @pl.when(unresolved)
        def _ties():
            def active(lo_, hi_, c_):
                mid = lo_ + (hi_ - lo_) * 0.5
                act = jnp.where(c_ != kf, jnp.where(mid > lo_, jnp.where(mid < hi_, 1.0, 0.0), 0.0), 0.0)
                return mid, act

            def cond(c):
                return c[4] > 0

            def wbody(c):
                lo_, hi_, c_, it, _ = c
                mid, act = active(lo_, hi_, c_)
                cnt = _count_ge(score_s, s_len, mid)
                up = jnp.where(cnt >= kf, act, 0.0) > 0.0
                dn = jnp.where(cnt >= kf, 0.0, act) > 0.0
                lo2 = jnp.where(up, mid, lo_)
                c2 = jnp.where(up, cnt, c_)
                hi2 = jnp.where(dn, mid, hi_)
                _, act2 = active(lo2, hi2, c2)
                go = jnp.where(jnp.max(act2) > 0.0, 1, 0) * jnp.where(it < 200, 1, 0)
                return lo2, hi2, c2, it + 1, go.astype(jnp.int32)

            _, act0 = active(lo, hi, cnt_lo)
            go0 = jnp.where(jnp.max(act0) > 0.0, 1, 0).astype(jnp.int32)
            lo_f, _, _, _, _ = lax.while_loop(cond, wbody, (lo, hi, cnt_lo, jnp.int32(0), go0))

            sc = score_s[:, 0:s_len]
            gt = sc > lo_f
            need = kf - jnp.sum(jnp.where(gt, 1.0, 0.0), axis=-1, keepdims=True)
            eq = jnp.where(sc == lo_f, 1.0, 0.0)
            rr = lax.broadcasted_iota(jnp.int32, (LANES, LANES), 0)
            cc = lax.broadcasted_iota(jnp.int32, (LANES, LANES), 1)
            upper = jnp.where(rr < cc, 1.0, 0.0).astype(BF16)
            carry = jnp.zeros((Q_BLOCK, 1), F32)
            for c0 in range(0, s_len, LANES):
                eq_c = eq[:, c0:c0 + LANES]
                prefix = _dot(eq_c.astype(BF16), upper) + carry
                keep = jnp.where(gt[:, c0:c0 + LANES], 1.0, jnp.where(prefix < need, eq_c, 0.0))
                bias_s[:, c0:c0 + LANES] = jnp.where(keep > 0.0, 0.0, NEG_INF)
                carry = carry + jnp.sum(eq_c, axis=-1, keepdims=True)

        key_pos = (s_idx - i * Q_BLOCK).astype(F32)
        for pr_i in range(B_HEADS // 2):
            qp = qn[:, pr_i * LANES:(pr_i + 1) * LANES]
            slopes = [2.0 ** (-8.0 * (2 * pr_i + e + 1) / B_HEADS) * LOG2E for e in range(2)]
            lgs = [_dot_nt(qp, kp_s[e, 0:s_len, :]) + slopes[e] * key_pos + bias_s[:, 0:s_len]
                   for e in range(2)]
            ms = [jnp.max(lg, axis=-1, keepdims=True) for lg in lgs]
            prs = [jnp.exp2(lg - m) for lg, m in zip(lgs, ms)]
            ls = [jnp.sum(pr, axis=-1, keepdims=True) for pr in prs]
            o_ref[:, pr_i * LANES:(pr_i + 1) * LANES] = (
                _dot(prs[0].astype(BF16), vp_s[0, 0:s_len, :]) / ls[0]
                + _dot(prs[1].astype(BF16), vp_s[1, 0:s_len, :]) / ls[1])

    per = (SEQ // Q_BLOCK) // len(buckets)
    for bi, s_len in enumerate(buckets):
        pl.when(i // per == bi)(functools.partial(body, s_len))


def _dsa(h0, gq_tile, gk_pad, gi_pad, buckets):
    h3 = h0.reshape(BATCH, SEQ, E_COLS)
    nq = SEQ // Q_BLOCK
    return pl.pallas_call(
        functools.partial(_dsa_kernel, buckets=buckets),
        grid=(BATCH, nq),
        in_specs=[
            pl.BlockSpec((None, Q_BLOCK, B_WIDTH), lambda b, i: (b, i, E_QB // B_WIDTH)),
            pl.BlockSpec((None, SEQ, LANES), lambda b, i: (b, 0, E_KV // LANES)),
            pl.BlockSpec((None, Q_BLOCK, 2 * LANES), lambda b, i: (b, i, E_QI // (2 * LANES))),
            pl.BlockSpec((None, SEQ, LANES), lambda b, i: (b, 0, E_KIW // LANES)),
            pl.BlockSpec((None, Q_BLOCK, LANES), lambda b, i: (b, i, E_KIW // LANES)),
            pl.BlockSpec((1, B_WIDTH), lambda b, i: (0, 0)),
            pl.BlockSpec((1, LANES), lambda b, i: (0, 0)),
            pl.BlockSpec((1, LANES), lambda b, i: (0, 0)),
        ],
        out_specs=pl.BlockSpec((None, Q_BLOCK, B_WIDTH), lambda b, i: (b, i, 0)),
        out_shape=jax.ShapeDtypeStruct((BATCH, SEQ, B_WIDTH), F32),
        scratch_shapes=[
            pltpu.VMEM((LANES // IDX_DIM, SEQ, LANES), BF16),
            pltpu.VMEM((2, SEQ, LANES), BF16),
            pltpu.VMEM((2, SEQ, LANES), BF16),
            pltpu.VMEM((Q_BLOCK, SEQ), F32),
            pltpu.VMEM((Q_BLOCK, SEQ), F32),
        ],
        compiler_params=_params("arbitrary", "arbitrary"),
        name="dsa",
    )(h3, h3, h3, h3, h3, gq_tile, gk_pad, gi_pad).reshape(BATCH * SEQ, B_WIDTH)


def _outproj_even_kernel(ya, za, yb, zb, x, w, out):
    ga = (ya[...] * _silu(za[...])).astype(BF16)
    gb = (yb[...] * _silu(zb[...])).astype(BF16)
    out[...] = x[...] + _dot(ga, w[0:A_WIDTH, :]) + _dot(gb, w[A_WIDTH:, :])


def _outproj_even(ya, h0, yb, x2d, w_bf16):
    m = x2d.shape[0]
    half = pl.BlockSpec((ROW_TILE, A_WIDTH), lambda i: (i, 0))
    return pl.pallas_call(
        _outproj_even_kernel,
        grid=(m // ROW_TILE,),
        in_specs=[
            half,
            pl.BlockSpec((ROW_TILE, A_WIDTH), lambda i: (i, E_ZA // A_WIDTH)),
            half,
            pl.BlockSpec((ROW_TILE, B_WIDTH), lambda i: (i, E_ZB // B_WIDTH)),
            pl.BlockSpec((ROW_TILE, D_MODEL), lambda i: (i, 0)),
            pl.BlockSpec((A_WIDTH + B_WIDTH, D_MODEL), lambda i: (0, 0)),
        ],
        out_specs=pl.BlockSpec((ROW_TILE, D_MODEL), lambda i: (i, 0)),
        out_shape=jax.ShapeDtypeStruct((m, D_MODEL), F32),
        compiler_params=_params("arbitrary"),
        name="outproj_even",
    )(ya, h0, yb, h0, x2d, w_bf16)


def _forget_kernel(fg_ref, b_ref, crow_ref):
    x = fg_ref[...] + b_ref[...]
    logf = jnp.minimum(x, 0.0) - jnp.log1p(jnp.exp(-jnp.abs(x)))
    rr = lax.broadcasted_iota(jnp.int32, (LANES, LANES), 0)
    cc = lax.broadcasted_iota(jnp.int32, (LANES, LANES), 1)
    tri = jnp.where(cc <= rr, 1.0, 0.0).astype(BF16)
    carry = jnp.zeros((1, LANES), F32)
    for blk in range(SEQ // LANES):
        rows = slice(blk * LANES, (blk + 1) * LANES)
        v = logf[rows, :]
        t0 = v.astype(BF16)
        r1 = v - t0.astype(F32)
        t1 = r1.astype(BF16)
        t2 = (r1 - t1.astype(F32)).astype(BF16)
        c = _dot(tri, t0) + _dot(tri, t1) + _dot(tri, t2) + carry
        crow_ref[:, rows] = c.T[0:C_HEADS, :]
        carry = c[LANES - 1:LANES, :]


def _forget_cumsum(h1, b_pad):
    h3 = h1.reshape(BATCH, SEQ, O_COLS)
    return pl.pallas_call(
        _forget_kernel,
        grid=(BATCH,),
        in_specs=[
            pl.BlockSpec((None, SEQ, LANES), lambda b: (b, 0, O_FG // LANES)),
            pl.BlockSpec((1, LANES), lambda b: (0, 0)),
        ],
        out_specs=pl.BlockSpec((None, C_HEADS, SEQ), lambda b: (b, 0, 0)),
        out_shape=jax.ShapeDtypeStruct((BATCH, C_HEADS, SEQ), F32),
        compiler_params=_params("arbitrary"),
        name="forget_cumsum",
    )(h3, b_pad)


def _fox_kernel(q_ref, k_ref, v_ref, crow_ref, gq_ref, gk_ref, y_ref, qn_s, kp_s, vp_s):
    lane = _lane_iota((1, LANES))
    low = lane < HEAD_DIM
    qn_s[...] = (_head_rms_norm(q_ref[...], gq_ref[...]) * (HEAD_DIM ** -0.5 * LOG2E)).astype(BF16)
    kn = _head_rms_norm(k_ref[...], gk_ref[...])
    v = v_ref[...]
    kp_s[0] = jnp.where(low, kn, 0.0).astype(BF16)
    kp_s[1] = jnp.where(low, 0.0, kn).astype(BF16)
    vp_s[0] = jnp.where(low, v, 0.0).astype(BF16)
    vp_s[1] = jnp.where(low, 0.0, v).astype(BF16)

    crow = crow_ref[...]
    on_or_below = (lax.broadcasted_iota(jnp.int32, (Q_BLOCK, Q_BLOCK), 1)
                   <= lax.broadcasted_iota(jnp.int32, (Q_BLOCK, Q_BLOCK), 0))
    for iq in range(SEQ // Q_BLOCK):
        rows = slice(iq * Q_BLOCK, (iq + 1) * Q_BLOCK)
        s_len = (iq + 1) * Q_BLOCK
        d0 = s_len - Q_BLOCK
        qblk = qn_s[rows, :]
        lgs = []
        for e in range(2):
            decay = (crow[e:e + 1, d0:d0 + 1] - crow[e:e + 1, 0:s_len]) * LOG2E
            lg = _dot_nt(qblk, kp_s[e, 0:s_len, :]) + decay
            tail = jnp.where(on_or_below, lg[:, d0:], NEG_INF)
            lgs.append(tail if d0 == 0 else jnp.concatenate([lg[:, :d0], tail], axis=-1))
        ms = [jnp.max(lg, axis=-1, keepdims=True) for lg in lgs]
        prs = [jnp.exp2(lg - m) for lg, m in zip(lgs, ms)]
        ls = [jnp.sum(pr, axis=-1, keepdims=True) for pr in prs]
        y_ref[rows, :] = (_dot(prs[0].astype(BF16), vp_s[0, 0:s_len, :]) / ls[0]
                          + _dot(prs[1].astype(BF16), vp_s[1, 0:s_len, :]) / ls[1])


def _fox(h1, crow, gq_tile, gk_tile):
    h3 = h1.reshape(BATCH, SEQ, O_COLS)
    crow4 = crow.reshape(BATCH, C_HEADS // 2, 2, SEQ)
    blk = lambda off: pl.BlockSpec((None, SEQ, LANES), lambda b, p: (b, 0, off // LANES + p))
    par = pl.BlockSpec((1, LANES), lambda b, p: (0, 0))
    return pl.pallas_call(
        _fox_kernel,
        grid=(BATCH, C_HEADS // 2),
        in_specs=[
            blk(O_Q), blk(O_K), blk(O_V),
            pl.BlockSpec((None, None, 2, SEQ), lambda b, p: (b, p, 0, 0)),
            par, par,
        ],
        out_specs=pl.BlockSpec((None, SEQ, LANES), lambda b, p: (b, 0, p)),
        out_shape=jax.ShapeDtypeStruct((BATCH, SEQ, C_WIDTH), F32),
        scratch_shapes=[
            pltpu.VMEM((SEQ, LANES), BF16),
            pltpu.VMEM((2, SEQ, LANES), BF16),
            pltpu.VMEM((2, SEQ, LANES), BF16),
        ],
        compiler_params=_params("arbitrary", "arbitrary"),
        name="fox",
    )(h3, h3, h3, crow4, gq_tile, gk_tile).reshape(BATCH * SEQ, C_WIDTH)


def _outproj_odd_kernel(y, z, x, w, out):
    g = (y[...] * _silu(z[...])).astype(BF16)
    out[...] = x[...] + _dot(g, w[...])


def _outproj_odd(y, h1, x2d, w_bf16):
    m = x2d.shape[0]
    full = pl.BlockSpec((ROW_TILE, D_MODEL), lambda i: (i, 0))
    return pl.pallas_call(
        _outproj_odd_kernel,
        grid=(m // ROW_TILE,),
        in_specs=[
            full,
            pl.BlockSpec((ROW_TILE, C_WIDTH), lambda i: (i, O_Z // C_WIDTH)),
            full,
            pl.BlockSpec((C_WIDTH, D_MODEL), lambda i: (0, 0)),
        ],
        out_specs=full,
        out_shape=jax.ShapeDtypeStruct((m, D_MODEL), F32),
        compiler_params=_params("arbitrary"),
        name="outproj_odd",
    )(y, h1, x2d, w_bf16)


def _even_weight(w):
    qa, ka, va, za, qb, kb, vb, zb, qi, ki, wi = jnp.split(
        w, np.cumsum([512, 512, 512, 512, 512, 64, 64, 512, 256, 32, 8])[:-1].tolist(), axis=-1)
    pad = jnp.zeros((w.shape[0], E_COLS - (E_KIW + IDX_DIM + IDX_HEADS)), w.dtype)
    return jnp.concatenate([qa, ka, va, za, qb, zb, qi, kb, vb, ki, wi, pad], axis=-1).astype(BF16)


def _pad_lanes(v, width):
    return jnp.pad(v, (0, width - v.shape[0])).reshape(1, width)


def _tile_heads(g, heads):
    return jnp.tile(g, heads).reshape(1, heads * HEAD_DIM)


def kernel(x, even_norm, even_w_in, even_q_norm_a, even_k_norm_a, even_q_norm_b, even_k_norm_b,
           even_k_norm_idx, even_w_out, odd_norm, odd_w_in, odd_b_forget, odd_q_norm, odd_k_norm,
           odd_w_out):
    assert x.shape == (BATCH, SEQ, D_MODEL)
    x2d = x.reshape(BATCH * SEQ, D_MODEL)

    h0 = _inproj(x2d, even_norm[0].reshape(1, D_MODEL), _even_weight(even_w_in[0]), chunk=512)
    slope_tile = jnp.asarray(
        np.repeat([2.0 ** (-8.0 * (i + 1) / A_HEADS) for i in range(A_HEADS)], HEAD_DIM), F32
    ).reshape(1, A_WIDTH)
    gqa = _tile_heads(even_q_norm_a[0], A_HEADS)
    gka = _tile_heads(even_k_norm_a[0], A_HEADS)
    ya = _dilated(h0, gqa, gka, slope_tile)
    yb = _dsa(
        h0,
        _tile_heads(even_q_norm_b[0], B_HEADS),
        _pad_lanes(even_k_norm_b[0], LANES),
        _pad_lanes(even_k_norm_idx[0], LANES),
        buckets=(512, 1024, 1536, 2048),
    )
    x1 = _outproj_even(ya, h0, yb, x2d, even_w_out[0].astype(BF16))

    w1 = jnp.pad(odd_w_in[0], ((0, 0), (0, O_COLS - odd_w_in.shape[-1]))).astype(BF16)
    h1 = _inproj(x1, odd_norm[0].reshape(1, D_MODEL), w1, chunk=384)
    crow = _forget_cumsum(h1, _pad_lanes(odd_b_forget[0], LANES))
    y = _fox(h1, crow, _tile_heads(odd_q_norm[0], 2), _tile_heads(odd_k_norm[0], 2))
    out = _outproj_odd(y, h1, x1, odd_w_out[0].astype(BF16))
    return out.reshape(BATCH, SEQ, D_MODEL)
```

```python
import functools

import numpy as np
import jax
import jax.numpy as jnp
from jax import lax
from jax.experimental import pallas as pl
from jax.experimental.pallas import tpu as pltpu

F32 = jnp.float32
BF16 = jnp.bfloat16

D_MODEL = 1024
BATCH = 16
SEQ = 2048
HEAD_DIM = 64
Q_BLOCK = 128
QUARTER = SEQ // 4
EPS = 1e-6
LANES = 128

A_HEADS = 8
A_GROUPS = ((128, 1), (512, 4), (2048, 16))
A_WIDTH = A_HEADS * HEAD_DIM
B_HEADS = 8
B_WIDTH = B_HEADS * HEAD_DIM
IDX_HEADS = 8
IDX_DIM = 32
IDX_TOPK = 256
C_HEADS = 16
C_WIDTH = C_HEADS * HEAD_DIM

E_QA, E_KA, E_VA, E_ZA = 0, 512, 1024, 1536
E_QB, E_ZB, E_QI, E_KV, E_KIW = 2048, 2560, 3072, 3328, 3456
E_COLS = 3584
O_Q, O_K, O_V, O_Z, O_FG = 0, 1024, 2048, 3072, 4096
O_COLS = 4224

ROW_TILE = 256
VMEM_LIMIT = 48 * 1024 * 1024
NEG_INF = float("-inf")
F32_LOWEST = float(np.finfo(np.float32).min)
INT_MIN = -(2 ** 31)
LOG2E = float(np.log2(np.e))
FIRST_PER_ITER = 4
REST_PER_ITER = {1: 5, 4: 4}
COUNT_SLAB = 64

_NT = (((1,), (1,)), ((), ()))


def _params(*sem):
    return pltpu.CompilerParams(dimension_semantics=sem, vmem_limit_bytes=VMEM_LIMIT)


def _dot(a, b):
    return jnp.dot(a, b, preferred_element_type=F32)


def _dot_nt(a, b):
    return lax.dot_general(a, b, _NT, preferred_element_type=F32)


def _lane_iota(shape):
    return lax.broadcasted_iota(jnp.int32, shape, len(shape) - 1)


def _segment_mean_sq(x, seg):
    r = lax.broadcasted_iota(jnp.int32, (LANES, LANES), 0) // seg
    c = lax.broadcasted_iota(jnp.int32, (LANES, LANES), 1) // seg
    bd = jnp.where(r == c, 1.0 / seg, 0.0).astype(BF16)
    xx = x * x
    hi = xx.astype(BF16)
    lo = (xx - hi.astype(F32)).astype(BF16)
    return _dot(hi, bd) + _dot(lo, bd)


def _head_rms_norm(x, g):
    w = x.shape[-1]
    tiles = []
    for c in range(0, w, LANES):
        xt = x[:, c:c + LANES]
        ms = _segment_mean_sq(xt, HEAD_DIM)
        tiles.append(xt * lax.rsqrt(ms + EPS) * g[:, c:c + LANES])
    return tiles[0] if len(tiles) == 1 else jnp.concatenate(tiles, axis=-1)


def _with_ones_lane(v_placed, lane, head):
    return jnp.where(lane == _ones_lane(head), 1.0, v_placed).astype(BF16)


def _ones_lane(head):
    return HEAD_DIM if head == 0 else 0


def _pair_output(prs, vp_s, s_len, low):
    pv = [_dot(prs[e].astype(BF16), vp_s[e, 0:s_len, :]) for e in range(2)]
    ls = [pv[e][:, _ones_lane(e):_ones_lane(e) + 1] for e in range(2)]
    return jnp.where(low, pv[0] / ls[0], pv[1] / ls[1])


def _silu(z):
    return z / (1.0 + jnp.exp(-z))


def _inproj_kernel(x_ref, g_ref, w_ref, o_ref, *, chunk):
    x = x_ref[...]
    ms = jnp.mean(x * x, axis=-1, keepdims=True)
    xn = (x * lax.rsqrt(ms + EPS) * g_ref[...]).astype(BF16)
    for c in range(0, o_ref.shape[-1], chunk):
        o_ref[:, c:c + chunk] = _dot(xn, w_ref[:, c:c + chunk])


def _inproj(x2d, g, w_bf16, chunk):
    m, d = x2d.shape
    n = w_bf16.shape[1]
    return pl.pallas_call(
        functools.partial(_inproj_kernel, chunk=chunk),
        grid=(m // ROW_TILE,),
        in_specs=[
            pl.BlockSpec((ROW_TILE, d), lambda i: (i, 0)),
            pl.BlockSpec((1, d), lambda i: (0, 0)),
            pl.BlockSpec((d, n), lambda i: (0, 0)),
        ],
        out_specs=pl.BlockSpec((ROW_TILE, n), lambda i: (i, 0)),
        out_shape=jax.ShapeDtypeStruct((m, n), F32),
        compiler_params=_params("arbitrary"),
        name="inproj",
    )(x2d, g, w_bf16)


def _rows(start, size, stride):
    return pl.ds(start, size) if stride == 1 else pl.ds(start, size, stride=stride)


def _dilated_kernel(q_ref, k_ref, v_ref, gq_ref, gk_ref, sl_ref, y_ref,
                    qn_s, kn_s, q4_s, k4_s, v4_s, pv_s, m_s, l_s):
    qn_s[...] = _head_rms_norm(q_ref[...], gq_ref[...]) * (HEAD_DIM ** -0.5 * LOG2E)
    kn_s[...] = _head_rms_norm(k_ref[...], gk_ref[...])
    for c in range(4):
        dst = slice(c * QUARTER, (c + 1) * QUARTER)
        src = _rows(c, QUARTER, 4)
        q4_s[dst, :] = qn_s[src, :]
        k4_s[dst, :] = kn_s[src, :]
        v4_s[dst, :] = v_ref[src, :]

    lane = _lane_iota((1, LANES))
    low = lane < HEAD_DIM
    qi = lax.broadcasted_iota(jnp.int32, (Q_BLOCK, 2 * Q_BLOCK), 0)
    kj = lax.broadcasted_iota(jnp.int32, (Q_BLOCK, 2 * Q_BLOCK), 1)
    dsub = Q_BLOCK + qi - kj
    slopes = sl_ref[...] * LOG2E
    slope_pair = (slopes[:, 0:1], slopes[:, HEAD_DIM:HEAD_DIM + 1])

    for g, (window, dil) in enumerate(A_GROUPS):
        span = window // dil
        nb = SEQ // dil // Q_BLOCK
        base = jnp.where(dsub >= 0, jnp.where(dsub <= span, (-dil * dsub).astype(F32), NEG_INF), NEG_INF)
        biases = (base * slope_pair[0], base * slope_pair[1])
        q_src, k_src, v_src = (qn_s, kn_s, v_ref) if g == 0 else (q4_s, k4_s, v4_s)
        stride = 4 if dil == 16 else 1

        def load(q0, first, stride=stride, q_src=q_src, k_src=k_src, v_src=v_src):
            qr = _rows(q0, Q_BLOCK, stride)
            kr = qr if first else _rows(q0 - stride * Q_BLOCK, 2 * Q_BLOCK, stride)
            return q_src[qr, :], k_src[kr, :].astype(BF16), v_src[kr, :].astype(BF16)

        def compute(loaded, first, biases=biases):
            n = len(loaded)
            lgs = [
                _dot_nt(jnp.where(low if e == 0 else ~low, qb, 0.0).astype(BF16), kk)
                + (biases[e][:, Q_BLOCK:] if first else biases[e])
                for qb, kk, _ in loaded for e in range(2)
            ]
            ms = [jnp.max(lg, axis=-1, keepdims=True) for lg in lgs]
            prs = [jnp.exp2(lg - m) for lg, m in zip(lgs, ms)]
            ls = [jnp.sum(pr, axis=-1, keepdims=True) for pr in prs]
            pvs = [_dot(pr.astype(BF16), loaded[i // 2][2]) for i, pr in enumerate(prs)]
            return [
                (jnp.where(low, pvs[2 * b], pvs[2 * b + 1]), jnp.where(low, ms[2 * b], ms[2 * b + 1]),
                 jnp.where(low, ls[2 * b], ls[2 * b + 1]))
                for b in range(n)
            ]

        def store(q0, res, g=g, stride=stride):
            qr = _rows(q0, Q_BLOCK, stride)
            pv_s[g, qr, :] = res[0]
            m_s[g, qr, :] = res[1]
            l_s[g, qr, :] = res[2]

        def run_blocks(start_of, count, per_iter, first, load=load, compute=compute, store=store):
            def body(t, carry):
                starts = [start_of(t * per_iter + u) for u in range(per_iter)]
                loaded = [load(q0, first) for q0 in starts]
                results = compute(loaded, first)
                for q0, res in zip(starts, results):
                    store(q0, res)
                return carry
            trips = count // per_iter
            if trips == 1:
                body(0, 0)
            else:
                lax.fori_loop(0, trips, body, 0)

        per = nb - 1
        if dil == 1:
            first_start = lambda idx: 0
            rest_start = lambda idx: Q_BLOCK * (1 + idx)
        elif dil == 4:
            first_start = lambda idx: idx * QUARTER
            rest_start = lambda idx, per=per: lax.div(idx, per) * QUARTER + Q_BLOCK * (1 + lax.rem(idx, per))
        else:
            first_start = lambda idx: lax.rem(idx, 4) * QUARTER + lax.div(idx, 4)
            rest_start = None
        run_blocks(first_start, dil, min(dil, FIRST_PER_ITER), True)
        if per:
            run_blocks(rest_start, dil * per, REST_PER_ITER[dil], False)

    chunk = 2 * Q_BLOCK
    for c in range(4):
        for o in range(0, QUARTER, chunk):
            nat = _rows(c + 4 * o, chunk, 4)
            grp = slice(c * QUARTER + o, c * QUARTER + o + chunk)
            m0, m1, m2 = m_s[0, nat, :], m_s[1, grp, :], m_s[2, grp, :]
            mm = jnp.maximum(jnp.maximum(m0, m1), m2)
            w0, w1, w2 = jnp.exp2(m0 - mm), jnp.exp2(m1 - mm), jnp.exp2(m2 - mm)
            num = w0 * pv_s[0, nat, :] + w1 * pv_s[1, grp, :] + w2 * pv_s[2, grp, :]
            den = w0 * l_s[0, nat, :] + w1 * l_s[1, grp, :] + w2 * l_s[2, grp, :]
            y_ref[nat, :] = num / den


def _dilated(h0, gq_tile, gk_tile, slope_tile):
    h3 = h0.reshape(BATCH, SEQ, E_COLS)
    blk = lambda off: pl.BlockSpec((None, SEQ, LANES), lambda b, p: (b, 0, off // LANES + p))
    par = pl.BlockSpec((1, LANES), lambda b, p: (0, p))
    seq_tile = pltpu.VMEM((SEQ, LANES), F32)
    grp_tile = pltpu.VMEM((len(A_GROUPS), SEQ, LANES), F32)
    return pl.pallas_call(
        _dilated_kernel,
        grid=(BATCH, A_WIDTH // LANES),
        in_specs=[blk(E_QA), blk(E_KA), blk(E_VA), par, par, par],
        out_specs=pl.BlockSpec((None, SEQ, LANES), lambda b, p: (b, 0, p)),
        out_shape=jax.ShapeDtypeStruct((BATCH, SEQ, A_WIDTH), F32),
        scratch_shapes=[seq_tile] * 5 + [grp_tile] * 3,
        compiler_params=_params("arbitrary", "arbitrary"),
        name="dilated",
    )(h3, h3, h3, gq_tile, gk_tile, slope_tile).reshape(BATCH * SEQ, A_WIDTH)


def _count_ge(score_ref, s_len, p):
    x = score_ref[:, 0:s_len]
    return jnp.sum(jnp.where(x >= p, 1.0, 0.0), axis=-1, keepdims=True)


def _key_to_float(key):
    bits = key ^ ((key >> 31) & 0x7FFFFFFF)
    return lax.bitcast_convert_type(bits, F32)


def _dsa_kernel(qb_ref, kv_ref, qi_ref, kiw_ref, wq_ref, gq_ref, gk_ref, gi_ref, o_ref,
                kip_s, kp_s, vp_s, score_s, score_t_s, bias_s, *, buckets):
    i = pl.program_id(1)
    lane = _lane_iota((1, LANES))
    kf = float(IDX_TOPK)

    @pl.when(i == 0)
    def _prep():
        kiw = kiw_ref[...]
        ki = jnp.where(lane < IDX_DIM, kiw, 0.0)
        ms = jnp.sum(ki * ki, axis=-1, keepdims=True) * (1.0 / IDX_DIM)
        kin = ki * lax.rsqrt(ms + EPS) * gi_ref[...]
        kip_s[0] = kin.astype(BF16)
        for c in range(1, LANES // IDX_DIM):
            kip_s[c] = pltpu.roll(kin, IDX_DIM * c, axis=1).astype(BF16)
        kv = kv_ref[...]
        kb = jnp.where(lane < HEAD_DIM, kv, 0.0)
        ms = jnp.sum(kb * kb, axis=-1, keepdims=True) * (1.0 / HEAD_DIM)
        kbn = kb * lax.rsqrt(ms + EPS) * gk_ref[...]
        kp_s[0] = kbn.astype(BF16)
        kp_s[1] = pltpu.roll(kbn, HEAD_DIM, axis=1).astype(BF16)
        vhi = jnp.where(lane >= HEAD_DIM, kv, 0.0)
        vp_s[1] = _with_ones_lane(vhi, lane, 1)
        vp_s[0] = _with_ones_lane(pltpu.roll(vhi, HEAD_DIM, axis=1), lane, 0)

    qn = (_head_rms_norm(qb_ref[...], gq_ref[...]) * (HEAD_DIM ** -0.5 * LOG2E)).astype(BF16)
    qidx = qi_ref[...].astype(BF16)
    wq = wq_ref[...] * (IDX_HEADS ** -0.5 * IDX_DIM ** -0.5)
    t_idx = i * Q_BLOCK + lax.broadcasted_iota(jnp.int32, (Q_BLOCK, 1), 0)

    def body(s_len):
        s_idx = lax.broadcasted_iota(jnp.int32, (1, s_len), 1)
        causal = s_idx <= t_idx

        score = jnp.zeros((Q_BLOCK, s_len), F32)
        for h in range(IDX_HEADS):
            t = h // 4
            sh = _dot_nt(qidx[:, t * LANES:(t + 1) * LANES], kip_s[h % 4, 0:s_len, :])
            score = score + wq[:, IDX_DIM + h:IDX_DIM + h + 1] * jnp.maximum(sh, 0.0)
        masked = jnp.where(causal, score, NEG_INF)
        score_s[:, 0:s_len] = masked
        for c0 in range(0, s_len, LANES):
            score_t_s[c0:c0 + LANES, :] = masked[:, c0:c0 + LANES].T

        def count_t(p_row):
            acc = jnp.zeros((COUNT_SLAB, Q_BLOCK), F32)
            for r0 in range(0, s_len, COUNT_SLAB):
                acc = acc + jnp.where(score_t_s[r0:r0 + COUNT_SLAB, :] >= p_row, 1.0, 0.0)
            return jnp.sum(acc, axis=0, keepdims=True)

        def step(it, tu):
            bit = lax.shift_left(jnp.int32(1), 31 - it)
            cand = tu | bit
            p = _key_to_float(cand ^ INT_MIN)
            return jnp.where(count_t(p) >= kf, cand, tu)

        tu = lax.fori_loop(0, 32, step, jnp.zeros((1, Q_BLOCK), jnp.int32))
        key = tu ^ INT_MIN
        short_row = (i * Q_BLOCK + lane + 1).astype(F32) <= kf
        lo_row = jnp.where(short_row, F32_LOWEST, _key_to_float(key))
        hi_row = _key_to_float(key + 1)
        cnt_lo_row = jnp.where(short_row, kf, count_t(lo_row))
        eye = (lax.broadcasted_iota(jnp.int32, (Q_BLOCK, Q_BLOCK), 0)
               == lax.broadcasted_iota(jnp.int32, (Q_BLOCK, Q_BLOCK), 1))

        def to_col(row):
            return jnp.sum(jnp.where(eye, row, 0.0), axis=-1, keepdims=True)

        lo = to_col(lo_row)
        bias_s[:, 0:s_len] = jnp.where(score_s[:, 0:s_len] >= lo, 0.0, NEG_INF)

        unresolved = jnp.max(jnp.where(cnt_lo_row != kf, 1.0, 0.0)) > 0.0

        @pl.when(unresolved)
        def _ties():
            hi = to_col(hi_row)
            cnt_lo = to_col(cnt_lo_row)

            def active(lo_, hi_, c_):
                mid = lo_ + (hi_ - lo_) * 0.5
                act = jnp.where(c_ != kf, jnp.where(mid > lo_, jnp.where(mid < hi_, 1.0, 0.0), 0.0), 0.0)
                return mid, act

            def cond(c):
                return c[4] > 0

            def wbody(c):
                lo_, hi_, c_, it, _ = c
                mid, act = active(lo_, hi_, c_)
                cnt = _count_ge(score_s, s_len, mid)
                up = jnp.where(cnt >= kf, act, 0.0) > 0.0
                dn = jnp.where(cnt >= kf, 0.0, act) > 0.0
                lo2 = jnp.where(up, mid, lo_)
                c2 = jnp.where(up, cnt, c_)
                hi2 = jnp.where(dn, mid, hi_)
                _, act2 = active(lo2, hi2, c2)
                go = jnp.where(jnp.max(act2) > 0.0, 1, 0) * jnp.where(it < 200, 1, 0)
                return lo2, hi2, c2, it + 1, go.astype(jnp.int32)

            _, act0 = active(lo, hi, cnt_lo)
            go0 = jnp.where(jnp.max(act0) > 0.0, 1, 0).astype(jnp.int32)
            lo_f, _, _, _, _ = lax.while_loop(cond, wbody, (lo, hi, cnt_lo, jnp.int32(0), go0))

            sc = score_s[:, 0:s_len]
            gt = sc > lo_f
            need = kf - jnp.sum(jnp.where(gt, 1.0, 0.0), axis=-1, keepdims=True)
            eq = jnp.where(sc == lo_f, 1.0, 0.0)
            rr = lax.broadcasted_iota(jnp.int32, (LANES, LANES), 0)
            cc = lax.broadcasted_iota(jnp.int32, (LANES, LANES), 1)
            upper = jnp.where(rr < cc, 1.0, 0.0).astype(BF16)
            carry = jnp.zeros((Q_BLOCK, 1), F32)
            for c0 in range(0, s_len, LANES):
                eq_c = eq[:, c0:c0 + LANES]
                prefix = _dot(eq_c.astype(BF16), upper) + carry
                keep = jnp.where(gt[:, c0:c0 + LANES], 1.0, jnp.where(prefix < need, eq_c, 0.0))
                bias_s[:, c0:c0 + LANES] = jnp.where(keep > 0.0, 0.0, NEG_INF)
                carry = carry + jnp.sum(eq_c, axis=-1, keepdims=True)

        key_pos = (s_idx - i * Q_BLOCK).astype(F32)
        for pr_i in range(B_HEADS // 2):
            qp = qn[:, pr_i * LANES:(pr_i + 1) * LANES]
            slopes = [2.0 ** (-8.0 * (2 * pr_i + e + 1) / B_HEADS) * LOG2E for e in range(2)]
            lgs = [_dot_nt(qp, kp_s[e, 0:s_len, :]) + slopes[e] * key_pos + bias_s[:, 0:s_len]
                   for e in range(2)]
            ms = [jnp.max(lg, axis=-1, keepdims=True) for lg in lgs]
            prs = [jnp.exp2(lg - m) for lg, m in zip(lgs, ms)]
            o_ref[:, pr_i * LANES:(pr_i + 1) * LANES] = _pair_output(prs, vp_s, s_len, lane < HEAD_DIM)

    per = (SEQ // Q_BLOCK) // len(buckets)
    for bi, s_len in enumerate(buckets):
        pl.when(i // per == bi)(functools.partial(body, s_len))


def _dsa(h0, gq_tile, gk_pad, gi_pad, buckets):
    h3 = h0.reshape(BATCH, SEQ, E_COLS)
    nq = SEQ // Q_BLOCK
    return pl.pallas_call(
        functools.partial(_dsa_kernel, buckets=buckets),
        grid=(BATCH, nq),
        in_specs=[
            pl.BlockSpec((None, Q_BLOCK, B_WIDTH), lambda b, i: (b, i, E_QB // B_WIDTH)),
            pl.BlockSpec((None, SEQ, LANES), lambda b, i: (b, 0, E_KV // LANES)),
            pl.BlockSpec((None, Q_BLOCK, 2 * LANES), lambda b, i: (b, i, E_QI // (2 * LANES))),
            pl.BlockSpec((None, SEQ, LANES), lambda b, i: (b, 0, E_KIW // LANES)),
            pl.BlockSpec((None, Q_BLOCK, LANES), lambda b, i: (b, i, E_KIW // LANES)),
            pl.BlockSpec((1, B_WIDTH), lambda b, i: (0, 0)),
            pl.BlockSpec((1, LANES), lambda b, i: (0, 0)),
            pl.BlockSpec((1, LANES), lambda b, i: (0, 0)),
        ],
        out_specs=pl.BlockSpec((None, Q_BLOCK, B_WIDTH), lambda b, i: (b, i, 0)),
        out_shape=jax.ShapeDtypeStruct((BATCH, SEQ, B_WIDTH), F32),
        scratch_shapes=[
            pltpu.VMEM((LANES // IDX_DIM, SEQ, LANES), BF16),
            pltpu.VMEM((2, SEQ, LANES), BF16),
            pltpu.VMEM((2, SEQ, LANES), BF16),
            pltpu.VMEM((Q_BLOCK, SEQ), F32),
            pltpu.VMEM((SEQ, Q_BLOCK), F32),
            pltpu.VMEM((Q_BLOCK, SEQ), F32),
        ],
        compiler_params=_params("arbitrary", "arbitrary"),
        name="dsa",
    )(h3, h3, h3, h3, h3, gq_tile, gk_pad, gi_pad).reshape(BATCH * SEQ, B_WIDTH)


def _outproj_even_kernel(ya, za, yb, zb, x, w, out):
    ga = (ya[...] * _silu(za[...])).astype(BF16)
    gb = (yb[...] * _silu(zb[...])).astype(BF16)
    out[...] = x[...] + _dot(ga, w[0:A_WIDTH, :]) + _dot(gb, w[A_WIDTH:, :])


def _outproj_even(ya, h0, yb, x2d, w_bf16):
    m = x2d.shape[0]
    half = pl.BlockSpec((ROW_TILE, A_WIDTH), lambda i: (i, 0))
    return pl.pallas_call(
        _outproj_even_kernel,
        grid=(m // ROW_TILE,),
        in_specs=[
            half,
            pl.BlockSpec((ROW_TILE, A_WIDTH), lambda i: (i, E_ZA // A_WIDTH)),
            half,
            pl.BlockSpec((ROW_TILE, B_WIDTH), lambda i: (i, E_ZB // B_WIDTH)),
            pl.BlockSpec((ROW_TILE, D_MODEL), lambda i: (i, 0)),
            pl.BlockSpec((A_WIDTH + B_WIDTH, D_MODEL), lambda i: (0, 0)),
        ],
        out_specs=pl.BlockSpec((ROW_TILE, D_MODEL), lambda i: (i, 0)),
        out_shape=jax.ShapeDtypeStruct((m, D_MODEL), F32),
        compiler_params=_params("arbitrary"),
        name="outproj_even",
    )(ya, h0, yb, h0, x2d, w_bf16)


def _forget_kernel(fg_ref, b_ref, crow_ref):
    x = fg_ref[...] + b_ref[...]
    logf = jnp.minimum(x, 0.0) - jnp.log1p(jnp.exp(-jnp.abs(x)))
    rr = lax.broadcasted_iota(jnp.int32, (LANES, LANES), 0)
    cc = lax.broadcasted_iota(jnp.int32, (LANES, LANES), 1)
    tri = jnp.where(cc <= rr, 1.0, 0.0).astype(BF16)
    carry = jnp.zeros((1, LANES), F32)
    for blk in range(SEQ // LANES):
        rows = slice(blk * LANES, (blk + 1) * LANES)
        v = logf[rows, :]
        t0 = v.astype(BF16)
        r1 = v - t0.astype(F32)
        t1 = r1.astype(BF16)
        t2 = (r1 - t1.astype(F32)).astype(BF16)
        c = _dot(tri, t0) + _dot(tri, t1) + _dot(tri, t2) + carry
        crow_ref[:, rows] = c.T[0:C_HEADS, :]
        carry = c[LANES - 1:LANES, :]


def _forget_cumsum(h1, b_pad):
    h3 = h1.reshape(BATCH, SEQ, O_COLS)
    return pl.pallas_call(
        _forget_kernel,
        grid=(BATCH,),
        in_specs=[
            pl.BlockSpec((None, SEQ, LANES), lambda b: (b, 0, O_FG // LANES)),
            pl.BlockSpec((1, LANES), lambda b: (0, 0)),
        ],
        out_specs=pl.BlockSpec((None, C_HEADS, SEQ), lambda b: (b, 0, 0)),
        out_shape=jax.ShapeDtypeStruct((BATCH, C_HEADS, SEQ), F32),
        compiler_params=_params("arbitrary"),
        name="forget_cumsum",
    )(h3, b_pad)


def _fox_kernel(q_ref, k_ref, v_ref, crow_ref, gq_ref, gk_ref, y_ref, qn_s, kp_s, vp_s):
    lane = _lane_iota((1, LANES))
    low = lane < HEAD_DIM
    qn_s[...] = (_head_rms_norm(q_ref[...], gq_ref[...]) * (HEAD_DIM ** -0.5 * LOG2E)).astype(BF16)
    kn = _head_rms_norm(k_ref[...], gk_ref[...])
    v = v_ref[...]
    kp_s[0] = jnp.where(low, kn, 0.0).astype(BF16)
    kp_s[1] = jnp.where(low, 0.0, kn).astype(BF16)
    vp_s[0] = _with_ones_lane(jnp.where(low, v, 0.0), lane, 0)
    vp_s[1] = _with_ones_lane(jnp.where(low, 0.0, v), lane, 1)

    crow = crow_ref[...]
    on_or_below = (lax.broadcasted_iota(jnp.int32, (Q_BLOCK, Q_BLOCK), 1)
                   <= lax.broadcasted_iota(jnp.int32, (Q_BLOCK, Q_BLOCK), 0))
    for iq in range(SEQ // Q_BLOCK):
        rows = slice(iq * Q_BLOCK, (iq + 1) * Q_BLOCK)
        s_len = (iq + 1) * Q_BLOCK
        d0 = s_len - Q_BLOCK
        qblk = qn_s[rows, :]
        lgs = []
        for e in range(2):
            decay = (crow[e:e + 1, d0:d0 + 1] - crow[e:e + 1, 0:s_len]) * LOG2E
            lg = _dot_nt(qblk, kp_s[e, 0:s_len, :]) + decay
            tail = jnp.where(on_or_below, lg[:, d0:], NEG_INF)
            lgs.append(tail if d0 == 0 else jnp.concatenate([lg[:, :d0], tail], axis=-1))
        ms = [jnp.max(lg, axis=-1, keepdims=True) for lg in lgs]
        prs = [jnp.exp2(lg - m) for lg, m in zip(lgs, ms)]
        y_ref[rows, :] = _pair_output(prs, vp_s, s_len, low)


def _fox(h1, crow, gq_tile, gk_tile):
    h3 = h1.reshape(BATCH, SEQ, O_COLS)
    crow4 = crow.reshape(BATCH, C_HEADS // 2, 2, SEQ)
    blk = lambda off: pl.BlockSpec((None, SEQ, LANES), lambda b, p: (b, 0, off // LANES + p))
    par = pl.BlockSpec((1, LANES), lambda b, p: (0, 0))
    return pl.pallas_call(
        _fox_kernel,
        grid=(BATCH, C_HEADS // 2),
        in_specs=[
            blk(O_Q), blk(O_K), blk(O_V),
            pl.BlockSpec((None, None, 2, SEQ), lambda b, p: (b, p, 0, 0)),
            par, par,
        ],
        out_specs=pl.BlockSpec((None, SEQ, LANES), lambda b, p: (b, 0, p)),
        out_shape=jax.ShapeDtypeStruct((BATCH, SEQ, C_WIDTH), F32),
        scratch_shapes=[
            pltpu.VMEM((SEQ, LANES), BF16),
            pltpu.VMEM((2, SEQ, LANES), BF16),
            pltpu.VMEM((2, SEQ, LANES), BF16),
        ],
        compiler_params=_params("arbitrary", "arbitrary"),
        name="fox",
    )(h3, h3, h3, crow4, gq_tile, gk_tile).reshape(BATCH * SEQ, C_WIDTH)


def _outproj_odd_kernel(y, z, x, w, out):
    g = (y[...] * _silu(z[...])).astype(BF16)
    out[...] = x[...] + _dot(g, w[...])


def _outproj_odd(y, h1, x2d, w_bf16):
    m = x2d.shape[0]
    full = pl.BlockSpec((ROW_TILE, D_MODEL), lambda i: (i, 0))
    return pl.pallas_call(
        _outproj_odd_kernel,
        grid=(m // ROW_TILE,),
        in_specs=[
            full,
            pl.BlockSpec((ROW_TILE, C_WIDTH), lambda i: (i, O_Z // C_WIDTH)),
            full,
            pl.BlockSpec((C_WIDTH, D_MODEL), lambda i: (0, 0)),
        ],
        out_specs=full,
        out_shape=jax.ShapeDtypeStruct((m, D_MODEL), F32),
        compiler_params=_params("arbitrary"),
        name="outproj_odd",
    )(y, h1, x2d, w_bf16)


def _even_weight(w):
    qa, ka, va, za, qb, kb, vb, zb, qi, ki, wi = jnp.split(
        w, np.cumsum([512, 512, 512, 512, 512, 64, 64, 512, 256, 32, 8])[:-1].tolist(), axis=-1)
    pad = jnp.zeros((w.shape[0], E_COLS - (E_KIW + IDX_DIM + IDX_HEADS)), w.dtype)
    return jnp.concatenate([qa, ka, va, za, qb, zb, qi, kb, vb, ki, wi, pad], axis=-1).astype(BF16)


def _pad_lanes(v, width):
    return jnp.pad(v, (0, width - v.shape[0])).reshape(1, width)


def _tile_heads(g, heads):
    return jnp.tile(g, heads).reshape(1, heads * HEAD_DIM)


def kernel(x, even_norm, even_w_in, even_q_norm_a, even_k_norm_a, even_q_norm_b, even_k_norm_b,
           even_k_norm_idx, even_w_out, odd_norm, odd_w_in, odd_b_forget, odd_q_norm, odd_k_norm,
           odd_w_out):
    assert x.shape == (BATCH, SEQ, D_MODEL)
    x2d = x.reshape(BATCH * SEQ, D_MODEL)

    h0 = _inproj(x2d, even_norm[0].reshape(1, D_MODEL), _even_weight(even_w_in[0]), chunk=512)
    slope_tile = jnp.asarray(
        np.repeat([2.0 ** (-8.0 * (i + 1) / A_HEADS) for i in range(A_HEADS)], HEAD_DIM), F32
    ).reshape(1, A_WIDTH)
    gqa = _tile_heads(even_q_norm_a[0], A_HEADS)
    gka = _tile_heads(even_k_norm_a[0], A_HEADS)
    ya = _dilated(h0, gqa, gka, slope_tile)
    yb = _dsa(
        h0,
        _tile_heads(even_q_norm_b[0], B_HEADS),
        _pad_lanes(even_k_norm_b[0], LANES),
        _pad_lanes(even_k_norm_idx[0], LANES),
        buckets=tuple(range(256, SEQ + 1, 256)),
    )
    x1 = _outproj_even(ya, h0, yb, x2d, even_w_out[0].astype(BF16))

    w1 = jnp.pad(odd_w_in[0], ((0, 0), (0, O_COLS - odd_w_in.shape[-1]))).astype(BF16)
    h1 = _inproj(x1, odd_norm[0].reshape(1, D_MODEL), w1, chunk=384)
    crow = _forget_cumsum(h1, _pad_lanes(odd_b_forget[0], LANES))
    y = _fox(h1, crow, _tile_heads(odd_q_norm[0], 2), _tile_heads(odd_k_norm[0], 2))
    out = _outproj_odd(y, h1, x1, odd_w_out[0].astype(BF16))
    return out.reshape(BATCH, SEQ, D_MODEL)
```

```python
import functools

import numpy as np
import jax
import jax.numpy as jnp
from jax import lax
from jax.experimental import pallas as pl
from jax.experimental.pallas import tpu as pltpu

F32 = jnp.float32
BF16 = jnp.bfloat16

D_MODEL = 1024
BATCH = 16
SEQ = 2048
HEAD_DIM = 64
Q_BLOCK = 128
QUARTER = SEQ // 4
EPS = 1e-6
LANES = 128

A_HEADS = 8
A_GROUPS = ((128, 1), (512, 4), (2048, 16))
A_WIDTH = A_HEADS * HEAD_DIM
B_HEADS = 8
B_WIDTH = B_HEADS * HEAD_DIM
IDX_HEADS = 8
IDX_DIM = 32
IDX_TOPK = 256
C_HEADS = 16
C_WIDTH = C_HEADS * HEAD_DIM

E_QA, E_KA, E_VA, E_ZA = 0, 512, 1024, 1536
E_QB, E_ZB, E_QI, E_KV, E_KIW = 2048, 2560, 3072, 3328, 3456
E_COLS = 3584
O_Q, O_K, O_V, O_Z, O_FG = 0, 1024, 2048, 3072, 4096
O_COLS = 4224

ROW_TILE = 256
VMEM_LIMIT = 48 * 1024 * 1024
NEG_INF = float("-inf")
F32_LOWEST = float(np.finfo(np.float32).min)
INT_MIN = -(2 ** 31)
LOG2E = float(np.log2(np.e))
FIRST_PER_ITER = 4
REST_PER_ITER = {1: 5, 4: 4}
FOX_Q_BLOCK = 256
COUNT_SLAB = 64

_NT = (((1,), (1,)), ((), ()))


def _params(*sem):
    return pltpu.CompilerParams(dimension_semantics=sem, vmem_limit_bytes=VMEM_LIMIT)


def _dot(a, b):
    return jnp.dot(a, b, preferred_element_type=F32)


def _dot_nt(a, b):
    return lax.dot_general(a, b, _NT, preferred_element_type=F32)


def _lane_iota(shape):
    return lax.broadcasted_iota(jnp.int32, shape, len(shape) - 1)


def _segment_mean_sq(x, seg):
    r = lax.broadcasted_iota(jnp.int32, (LANES, LANES), 0) // seg
    c = lax.broadcasted_iota(jnp.int32, (LANES, LANES), 1) // seg
    bd = jnp.where(r == c, 1.0 / seg, 0.0).astype(BF16)
    xx = x * x
    hi = xx.astype(BF16)
    lo = (xx - hi.astype(F32)).astype(BF16)
    return _dot(hi, bd) + _dot(lo, bd)


def _head_rms_norm(x, g):
    w = x.shape[-1]
    tiles = []
    for c in range(0, w, LANES):
        xt = x[:, c:c + LANES]
        ms = _segment_mean_sq(xt, HEAD_DIM)
        tiles.append(xt * lax.rsqrt(ms + EPS) * g[:, c:c + LANES])
    return tiles[0] if len(tiles) == 1 else jnp.concatenate(tiles, axis=-1)


def _with_ones_lane(v_placed, lane, head):
    return jnp.where(lane == _ones_lane(head), 1.0, v_placed).astype(BF16)


def _ones_lane(head):
    return HEAD_DIM if head == 0 else 0


def _pair_output(prs, vp_s, s_len, low):
    pv = [_dot(prs[e].astype(BF16), vp_s[e, 0:s_len, :]) for e in range(2)]
    ls = [pv[e][:, _ones_lane(e):_ones_lane(e) + 1] for e in range(2)]
    return jnp.where(low, pv[0] / ls[0], pv[1] / ls[1])


def _silu(z):
    return z / (1.0 + jnp.exp(-z))


def _inproj_kernel(x_ref, g_ref, w_ref, o_ref, *, chunk):
    x = x_ref[...]
    ms = jnp.mean(x * x, axis=-1, keepdims=True)
    xn = (x * lax.rsqrt(ms + EPS) * g_ref[...]).astype(BF16)
    for c in range(0, o_ref.shape[-1], chunk):
        o_ref[:, c:c + chunk] = _dot(xn, w_ref[:, c:c + chunk])


def _inproj(x2d, g, w_bf16, chunk):
    m, d = x2d.shape
    n = w_bf16.shape[1]
    return pl.pallas_call(
        functools.partial(_inproj_kernel, chunk=chunk),
        grid=(m // ROW_TILE,),
        in_specs=[
            pl.BlockSpec((ROW_TILE, d), lambda i: (i, 0)),
            pl.BlockSpec((1, d), lambda i: (0, 0)),
            pl.BlockSpec((d, n), lambda i: (0, 0)),
        ],
        out_specs=pl.BlockSpec((ROW_TILE, n), lambda i: (i, 0)),
        out_shape=jax.ShapeDtypeStruct((m, n), F32),
        compiler_params=_params("arbitrary"),
        name="inproj",
    )(x2d, g, w_bf16)


def _rows(start, size, stride):
    return pl.ds(start, size) if stride == 1 else pl.ds(start, size, stride=stride)


def _dilated_kernel(q_ref, k_ref, v_ref, gq_ref, gk_ref, sl_ref, y_ref,
                    qn_s, kn_s, q4_s, k4_s, v4_s, pv_s, m_s, l_s):
    qn_s[...] = _head_rms_norm(q_ref[...], gq_ref[...]) * (HEAD_DIM ** -0.5 * LOG2E)
    kn_s[...] = _head_rms_norm(k_ref[...], gk_ref[...])
    for c in range(4):
        dst = slice(c * QUARTER, (c + 1) * QUARTER)
        src = _rows(c, QUARTER, 4)
        q4_s[dst, :] = qn_s[src, :]
        k4_s[dst, :] = kn_s[src, :]
        v4_s[dst, :] = v_ref[src, :]

    lane = _lane_iota((1, LANES))
    low = lane < HEAD_DIM
    qi = lax.broadcasted_iota(jnp.int32, (Q_BLOCK, 2 * Q_BLOCK), 0)
    kj = lax.broadcasted_iota(jnp.int32, (Q_BLOCK, 2 * Q_BLOCK), 1)
    dsub = Q_BLOCK + qi - kj
    slopes = sl_ref[...] * LOG2E
    slope_pair = (slopes[:, 0:1], slopes[:, HEAD_DIM:HEAD_DIM + 1])

    for g, (window, dil) in enumerate(A_GROUPS):
        span = window // dil
        nb = SEQ // dil // Q_BLOCK
        base = jnp.where(dsub >= 0, jnp.where(dsub <= span, (-dil * dsub).astype(F32), NEG_INF), NEG_INF)
        biases = (base * slope_pair[0], base * slope_pair[1])
        q_src, k_src, v_src = (qn_s, kn_s, v_ref) if g == 0 else (q4_s, k4_s, v4_s)
        stride = 4 if dil == 16 else 1

        def load(q0, first, stride=stride, q_src=q_src, k_src=k_src, v_src=v_src):
            qr = _rows(q0, Q_BLOCK, stride)
            kr = qr if first else _rows(q0 - stride * Q_BLOCK, 2 * Q_BLOCK, stride)
            return q_src[qr, :], k_src[kr, :].astype(BF16), v_src[kr, :].astype(BF16)

        def compute(loaded, first, biases=biases):
            n = len(loaded)
            lgs = [
                _dot_nt(jnp.where(low if e == 0 else ~low, qb, 0.0).astype(BF16), kk)
                + (biases[e][:, Q_BLOCK:] if first else biases[e])
                for qb, kk, _ in loaded for e in range(2)
            ]
            ms = [jnp.max(lg, axis=-1, keepdims=True) for lg in lgs]
            prs = [jnp.exp2(lg - m) for lg, m in zip(lgs, ms)]
            ls = [jnp.sum(pr, axis=-1, keepdims=True) for pr in prs]
            pvs = [_dot(pr.astype(BF16), loaded[i // 2][2]) for i, pr in enumerate(prs)]
            return [
                (jnp.where(low, pvs[2 * b], pvs[2 * b + 1]), jnp.where(low, ms[2 * b], ms[2 * b + 1]),
                 jnp.where(low, ls[2 * b], ls[2 * b + 1]))
                for b in range(n)
            ]

        def store(q0, res, g=g, stride=stride):
            qr = _rows(q0, Q_BLOCK, stride)
            pv_s[g, qr, :] = res[0]
            m_s[g, qr, :] = res[1]
            l_s[g, qr, :] = res[2]

        def run_blocks(start_of, count, per_iter, first, load=load, compute=compute, store=store):
            def body(t, carry):
                starts = [start_of(t * per_iter + u) for u in range(per_iter)]
                loaded = [load(q0, first) for q0 in starts]
                results = compute(loaded, first)
                for q0, res in zip(starts, results):
                    store(q0, res)
                return carry
            trips = count // per_iter
            if trips == 1:
                body(0, 0)
            else:
                lax.fori_loop(0, trips, body, 0)

        per = nb - 1
        if dil == 1:
            first_start = lambda idx: 0
            rest_start = lambda idx: Q_BLOCK * (1 + idx)
        elif dil == 4:
            first_start = lambda idx: idx * QUARTER
            rest_start = lambda idx, per=per: lax.div(idx, per) * QUARTER + Q_BLOCK * (1 + lax.rem(idx, per))
        else:
            first_start = lambda idx: lax.rem(idx, 4) * QUARTER + lax.div(idx, 4)
            rest_start = None
        run_blocks(first_start, dil, min(dil, FIRST_PER_ITER), True)
        if per:
            run_blocks(rest_start, dil * per, REST_PER_ITER[dil], False)

    chunk = 2 * Q_BLOCK
    for c in range(4):
        for o in range(0, QUARTER, chunk):
            nat = _rows(c + 4 * o, chunk, 4)
            grp = slice(c * QUARTER + o, c * QUARTER + o + chunk)
            m0, m1, m2 = m_s[0, nat, :], m_s[1, grp, :], m_s[2, grp, :]
            mm = jnp.maximum(jnp.maximum(m0, m1), m2)
            w0, w1, w2 = jnp.exp2(m0 - mm), jnp.exp2(m1 - mm), jnp.exp2(m2 - mm)
            num = w0 * pv_s[0, nat, :] + w1 * pv_s[1, grp, :] + w2 * pv_s[2, grp, :]
            den = w0 * l_s[0, nat, :] + w1 * l_s[1, grp, :] + w2 * l_s[2, grp, :]
            y_ref[nat, :] = num / den


def _dilated(h0, gq_tile, gk_tile, slope_tile):
    h3 = h0.reshape(BATCH, SEQ, E_COLS)
    blk = lambda off: pl.BlockSpec((None, SEQ, LANES), lambda b, p: (b, 0, off // LANES + p))
    par = pl.BlockSpec((1, LANES), lambda b, p: (0, p))
    seq_tile = pltpu.VMEM((SEQ, LANES), F32)
    grp_tile = pltpu.VMEM((len(A_GROUPS), SEQ, LANES), F32)
    return pl.pallas_call(
        _dilated_kernel,
        grid=(BATCH, A_WIDTH // LANES),
        in_specs=[blk(E_QA), blk(E_KA), blk(E_VA), par, par, par],
        out_specs=pl.BlockSpec((None, SEQ, LANES), lambda b, p: (b, 0, p)),
        out_shape=jax.ShapeDtypeStruct((BATCH, SEQ, A_WIDTH), F32),
        scratch_shapes=[seq_tile] * 5 + [grp_tile] * 3,
        compiler_params=_params("arbitrary", "arbitrary"),
        name="dilated",
    )(h3, h3, h3, gq_tile, gk_tile, slope_tile).reshape(BATCH * SEQ, A_WIDTH)


def _count_ge(score_ref, s_len, p):
    x = score_ref[:, 0:s_len]
    return jnp.sum(jnp.where(x >= p, 1.0, 0.0), axis=-1, keepdims=True)


def _key_to_float(key):
    bits = key ^ ((key >> 31) & 0x7FFFFFFF)
    return lax.bitcast_convert_type(bits, F32)


def _dsa_kernel(qb_ref, kv_ref, qi_ref, kiw_ref, wq_ref, gq_ref, gk_ref, gi_ref, o_ref,
                ki_t_s, kb_t_s, v_s, score_s, score_t_s, bias_s, *, buckets):
    i = pl.program_id(1)
    lane = _lane_iota((1, LANES))
    kf = float(IDX_TOPK)

    @pl.when(i == 0)
    def _prep():
        kiw = kiw_ref[...]
        ki = jnp.where(lane < IDX_DIM, kiw, 0.0)
        ms = jnp.sum(ki * ki, axis=-1, keepdims=True) * (1.0 / IDX_DIM)
        kin = ki * lax.rsqrt(ms + EPS) * gi_ref[...]
        kv = kv_ref[...]
        kb = jnp.where(lane < HEAD_DIM, kv, 0.0)
        ms = jnp.sum(kb * kb, axis=-1, keepdims=True) * (1.0 / HEAD_DIM)
        kbn = kb * lax.rsqrt(ms + EPS) * gk_ref[...]
        for c0 in range(0, SEQ, LANES):
            ki_t_s[:, c0:c0 + LANES] = kin[c0:c0 + LANES, :].T[0:IDX_DIM, :].astype(BF16)
            kb_t_s[:, c0:c0 + LANES] = kbn[c0:c0 + LANES, :].T[0:HEAD_DIM, :].astype(BF16)
        v_s[...] = _with_ones_lane(jnp.where(lane >= HEAD_DIM, kv, 0.0), lane, 1)

    qn = _head_rms_norm(qb_ref[...], gq_ref[...]) * (HEAD_DIM ** -0.5 * LOG2E)
    q_heads = [qn[:, h * HEAD_DIM:(h + 1) * HEAD_DIM] for h in range(B_HEADS)]
    qraw = qi_ref[...]
    qi_all = jnp.concatenate(
        [qraw[:, h * IDX_DIM:(h + 1) * IDX_DIM] for h in range(IDX_HEADS)], axis=0).astype(BF16)
    wq = wq_ref[...] * (IDX_HEADS ** -0.5 * IDX_DIM ** -0.5)
    t_idx = i * Q_BLOCK + lax.broadcasted_iota(jnp.int32, (Q_BLOCK, 1), 0)

    def body(s_len):
        s_idx = lax.broadcasted_iota(jnp.int32, (1, s_len), 1)
        causal = s_idx <= t_idx

        sh_all = _dot(qi_all, ki_t_s[:, 0:s_len])
        score = jnp.zeros((Q_BLOCK, s_len), F32)
        for h in range(IDX_HEADS):
            sh = sh_all[h * Q_BLOCK:(h + 1) * Q_BLOCK, :]
            score = score + wq[:, IDX_DIM + h:IDX_DIM + h + 1] * jnp.maximum(sh, 0.0)
        masked = jnp.where(causal, score, NEG_INF)
        score_s[:, 0:s_len] = masked
        for c0 in range(0, s_len, LANES):
            score_t_s[c0:c0 + LANES, :] = masked[:, c0:c0 + LANES].T

        def count_t(p_row):
            acc = jnp.zeros((COUNT_SLAB, Q_BLOCK), F32)
            for r0 in range(0, s_len, COUNT_SLAB):
                acc = acc + jnp.where(score_t_s[r0:r0 + COUNT_SLAB, :] >= p_row, 1.0, 0.0)
            return jnp.sum(acc, axis=0, keepdims=True)

        def step(it, tu):
            bit = lax.shift_left(jnp.int32(1), 31 - it)
            cand = tu | bit
            p = _key_to_float(cand ^ INT_MIN)
            return jnp.where(count_t(p) >= kf, cand, tu)

        tu = lax.fori_loop(0, 32, step, jnp.zeros((1, Q_BLOCK), jnp.int32))
        key = tu ^ INT_MIN
        short_row = (i * Q_BLOCK + lane + 1).astype(F32) <= kf
        lo_row = jnp.where(short_row, F32_LOWEST, _key_to_float(key))
        hi_row = _key_to_float(key + 1)
        cnt_lo_row = jnp.where(short_row, kf, count_t(lo_row))
        eye = (lax.broadcasted_iota(jnp.int32, (Q_BLOCK, Q_BLOCK), 0)
               == lax.broadcasted_iota(jnp.int32, (Q_BLOCK, Q_BLOCK), 1))

        def to_col(row):
            return jnp.sum(jnp.where(eye, row, 0.0), axis=-1, keepdims=True)

        lo = to_col(lo_row)
        bias_s[:, 0:s_len] = jnp.where(score_s[:, 0:s_len] >= lo, 0.0, NEG_INF)

        unresolved = jnp.max(jnp.where(cnt_lo_row != kf, 1.0, 0.0)) > 0.0

        @pl.when(unresolved)
        def _ties():
            hi = to_col(hi_row)
            cnt_lo = to_col(cnt_lo_row)

            def active(lo_, hi_, c_):
                mid = lo_ + (hi_ - lo_) * 0.5
                act = jnp.where(c_ != kf, jnp.where(mid > lo_, jnp.where(mid < hi_, 1.0, 0.0), 0.0), 0.0)
                return mid, act

            def cond(c):
                return c[4] > 0

            def wbody(c):
                lo_, hi_, c_, it, _ = c
                mid, act = active(lo_, hi_, c_)
                cnt = _count_ge(score_s, s_len, mid)
                up = jnp.where(cnt >= kf, act, 0.0) > 0.0
                dn = jnp.where(cnt >= kf, 0.0, act) > 0.0
                lo2 = jnp.where(up, mid, lo_)
                c2 = jnp.where(up, cnt, c_)
                hi2 = jnp.where(dn, mid, hi_)
                _, act2 = active(lo2, hi2, c2)
                go = jnp.where(jnp.max(act2) > 0.0, 1, 0) * jnp.where(it < 200, 1, 0)
                return lo2, hi2, c2, it + 1, go.astype(jnp.int32)

            _, act0 = active(lo, hi, cnt_lo)
            go0 = jnp.where(jnp.max(act0) > 0.0, 1, 0).astype(jnp.int32)
            lo_f, _, _, _, _ = lax.while_loop(cond, wbody, (lo, hi, cnt_lo, jnp.int32(0), go0))

            sc = score_s[:, 0:s_len]
            gt = sc > lo_f
            need = kf - jnp.sum(jnp.where(gt, 1.0, 0.0), axis=-1, keepdims=True)
            eq = jnp.where(sc == lo_f, 1.0, 0.0)
            rr = lax.broadcasted_iota(jnp.int32, (LANES, LANES), 0)
            cc = lax.broadcasted_iota(jnp.int32, (LANES, LANES), 1)
            upper = jnp.where(rr < cc, 1.0, 0.0).astype(BF16)
            carry = jnp.zeros((Q_BLOCK, 1), F32)
            for c0 in range(0, s_len, LANES):
                eq_c = eq[:, c0:c0 + LANES]
                prefix = _dot(eq_c.astype(BF16), upper) + carry
                keep = jnp.where(gt[:, c0:c0 + LANES], 1.0, jnp.where(prefix < need, eq_c, 0.0))
                bias_s[:, c0:c0 + LANES] = jnp.where(keep > 0.0, 0.0, NEG_INF)
                carry = carry + jnp.sum(eq_c, axis=-1, keepdims=True)

        key_pos = (s_idx - i * Q_BLOCK).astype(F32)
        low = lane < HEAD_DIM
        for pr_i in range(B_HEADS // 2):
            q2 = jnp.concatenate(q_heads[2 * pr_i:2 * pr_i + 2], axis=0).astype(BF16)
            qk = _dot(q2, kb_t_s[:, 0:s_len])
            slopes = [2.0 ** (-8.0 * (2 * pr_i + e + 1) / B_HEADS) * LOG2E for e in range(2)]
            lgs = [qk[e * Q_BLOCK:(e + 1) * Q_BLOCK, :] + slopes[e] * key_pos + bias_s[:, 0:s_len]
                   for e in range(2)]
            ms = [jnp.max(lg, axis=-1, keepdims=True) for lg in lgs]
            prs = [jnp.exp2(lg - m).astype(BF16) for lg, m in zip(lgs, ms)]
            pv = _dot(jnp.concatenate(prs, axis=0), v_s[0:s_len, :])
            outs = [pv[e * Q_BLOCK:(e + 1) * Q_BLOCK, :] for e in range(2)]
            outs = [o / o[:, 0:1] for o in outs]
            o_ref[:, pr_i * LANES:(pr_i + 1) * LANES] = jnp.where(
                low, pltpu.roll(outs[0], HEAD_DIM, axis=1), outs[1])

    per = (SEQ // Q_BLOCK) // len(buckets)
    for bi, s_len in enumerate(buckets):
        pl.when(i // per == bi)(functools.partial(body, s_len))


def _dsa(h0, gq_tile, gk_pad, gi_pad, buckets):
    h3 = h0.reshape(BATCH, SEQ, E_COLS)
    nq = SEQ // Q_BLOCK
    return pl.pallas_call(
        functools.partial(_dsa_kernel, buckets=buckets),
        grid=(BATCH, nq),
        in_specs=[
            pl.BlockSpec((None, Q_BLOCK, B_WIDTH), lambda b, i: (b, i, E_QB // B_WIDTH)),
            pl.BlockSpec((None, SEQ, LANES), lambda b, i: (b, 0, E_KV // LANES)),
            pl.BlockSpec((None, Q_BLOCK, 2 * LANES), lambda b, i: (b, i, E_QI // (2 * LANES))),
            pl.BlockSpec((None, SEQ, LANES), lambda b, i: (b, 0, E_KIW // LANES)),
            pl.BlockSpec((None, Q_BLOCK, LANES), lambda b, i: (b, i, E_KIW // LANES)),
            pl.BlockSpec((1, B_WIDTH), lambda b, i: (0, 0)),
            pl.BlockSpec((1, LANES), lambda b, i: (0, 0)),
            pl.BlockSpec((1, LANES), lambda b, i: (0, 0)),
        ],
        out_specs=pl.BlockSpec((None, Q_BLOCK, B_WIDTH), lambda b, i: (b, i, 0)),
        out_shape=jax.ShapeDtypeStruct((BATCH, SEQ, B_WIDTH), F32),
        scratch_shapes=[
            pltpu.VMEM((IDX_DIM, SEQ), BF16),
            pltpu.VMEM((HEAD_DIM, SEQ), BF16),
            pltpu.VMEM((SEQ, LANES), BF16),
            pltpu.VMEM((Q_BLOCK, SEQ), F32),
            pltpu.VMEM((SEQ, Q_BLOCK), F32),
            pltpu.VMEM((Q_BLOCK, SEQ), F32),
        ],
        compiler_params=_params("arbitrary", "arbitrary"),
        name="dsa",
    )(h3, h3, h3, h3, h3, gq_tile, gk_pad, gi_pad).reshape(BATCH * SEQ, B_WIDTH)


def _outproj_even_kernel(ya, za, yb, zb, x, w, out):
    ga = (ya[...] * _silu(za[...])).astype(BF16)
    gb = (yb[...] * _silu(zb[...])).astype(BF16)
    out[...] = x[...] + _dot(ga, w[0:A_WIDTH, :]) + _dot(gb, w[A_WIDTH:, :])


def _outproj_even(ya, h0, yb, x2d, w_bf16):
    m = x2d.shape[0]
    half = pl.BlockSpec((ROW_TILE, A_WIDTH), lambda i: (i, 0))
    return pl.pallas_call(
        _outproj_even_kernel,
        grid=(m // ROW_TILE,),
        in_specs=[
            half,
            pl.BlockSpec((ROW_TILE, A_WIDTH), lambda i: (i, E_ZA // A_WIDTH)),
            half,
            pl.BlockSpec((ROW_TILE, B_WIDTH), lambda i: (i, E_ZB // B_WIDTH)),
            pl.BlockSpec((ROW_TILE, D_MODEL), lambda i: (i, 0)),
            pl.BlockSpec((A_WIDTH + B_WIDTH, D_MODEL), lambda i: (0, 0)),
        ],
        out_specs=pl.BlockSpec((ROW_TILE, D_MODEL), lambda i: (i, 0)),
        out_shape=jax.ShapeDtypeStruct((m, D_MODEL), F32),
        compiler_params=_params("arbitrary"),
        name="outproj_even",
    )(ya, h0, yb, h0, x2d, w_bf16)


def _forget_kernel(fg_ref, b_ref, crow_ref):
    x = fg_ref[...] + b_ref[...]
    logf = jnp.minimum(x, 0.0) - jnp.log1p(jnp.exp(-jnp.abs(x)))
    rr = lax.broadcasted_iota(jnp.int32, (LANES, LANES), 0)
    cc = lax.broadcasted_iota(jnp.int32, (LANES, LANES), 1)
    tri = jnp.where(cc <= rr, 1.0, 0.0).astype(BF16)
    carry = jnp.zeros((1, LANES), F32)
    for blk in range(SEQ // LANES):
        rows = slice(blk * LANES, (blk + 1) * LANES)
        v = logf[rows, :]
        t0 = v.astype(BF16)
        r1 = v - t0.astype(F32)
        t1 = r1.astype(BF16)
        t2 = (r1 - t1.astype(F32)).astype(BF16)
        c = _dot(tri, t0) + _dot(tri, t1) + _dot(tri, t2) + carry
        crow_ref[:, rows] = c.T[0:C_HEADS, :]
        carry = c[LANES - 1:LANES, :]


def _forget_cumsum(h1, b_pad):
    h3 = h1.reshape(BATCH, SEQ, O_COLS)
    return pl.pallas_call(
        _forget_kernel,
        grid=(BATCH,),
        in_specs=[
            pl.BlockSpec((None, SEQ, LANES), lambda b: (b, 0, O_FG // LANES)),
            pl.BlockSpec((1, LANES), lambda b: (0, 0)),
        ],
        out_specs=pl.BlockSpec((None, C_HEADS, SEQ), lambda b: (b, 0, 0)),
        out_shape=jax.ShapeDtypeStruct((BATCH, C_HEADS, SEQ), F32),
        compiler_params=_params("arbitrary"),
        name="forget_cumsum",
    )(h3, b_pad)


def _fox_kernel(q_ref, k_ref, v_ref, crow_ref, gq_ref, gk_ref, y_ref, qn_s, kp_t_s, vp_s):
    lane = _lane_iota((1, LANES))
    low = lane < HEAD_DIM
    qn_s[...] = (_head_rms_norm(q_ref[...], gq_ref[...]) * (HEAD_DIM ** -0.5 * LOG2E)).astype(BF16)
    kn = _head_rms_norm(k_ref[...], gk_ref[...])
    v = v_ref[...]
    for c0 in range(0, SEQ, LANES):
        kc = kn[c0:c0 + LANES, :]
        kp_t_s[0, :, c0:c0 + LANES] = jnp.where(low, kc, 0.0).T.astype(BF16)
        kp_t_s[1, :, c0:c0 + LANES] = jnp.where(low, 0.0, kc).T.astype(BF16)
    vp_s[0] = _with_ones_lane(jnp.where(low, v, 0.0), lane, 0)
    vp_s[1] = _with_ones_lane(jnp.where(low, 0.0, v), lane, 1)

    crow = crow_ref[...]
    on_or_below = (lax.broadcasted_iota(jnp.int32, (FOX_Q_BLOCK, FOX_Q_BLOCK), 1)
                   <= lax.broadcasted_iota(jnp.int32, (FOX_Q_BLOCK, FOX_Q_BLOCK), 0))
    for iq in range(SEQ // FOX_Q_BLOCK):
        rows = slice(iq * FOX_Q_BLOCK, (iq + 1) * FOX_Q_BLOCK)
        s_len = (iq + 1) * FOX_Q_BLOCK
        d0 = s_len - FOX_Q_BLOCK
        qblk = qn_s[rows, :]
        lgs = []
        for e in range(2):
            decay = (crow[e:e + 1, d0:d0 + 1] - crow[e:e + 1, 0:s_len]) * LOG2E
            lg = _dot(qblk, kp_t_s[e, :, 0:s_len]) + decay
            tail = jnp.where(on_or_below, lg[:, d0:], NEG_INF)
            lgs.append(tail if d0 == 0 else jnp.concatenate([lg[:, :d0], tail], axis=-1))
        ms = [jnp.max(lg, axis=-1, keepdims=True) for lg in lgs]
        prs = [jnp.exp2(lg - m) for lg, m in zip(lgs, ms)]
        y_ref[rows, :] = _pair_output(prs, vp_s, s_len, low)


def _fox(h1, crow, gq_tile, gk_tile):
    h3 = h1.reshape(BATCH, SEQ, O_COLS)
    crow4 = crow.reshape(BATCH, C_HEADS // 2, 2, SEQ)
    blk = lambda off: pl.BlockSpec((None, SEQ, LANES), lambda b, p: (b, 0, off // LANES + p))
    par = pl.BlockSpec((1, LANES), lambda b, p: (0, 0))
    return pl.pallas_call(
        _fox_kernel,
        grid=(BATCH, C_HEADS // 2),
        in_specs=[
            blk(O_Q), blk(O_K), blk(O_V),
            pl.BlockSpec((None, None, 2, SEQ), lambda b, p: (b, p, 0, 0)),
            par, par,
        ],
        out_specs=pl.BlockSpec((None, SEQ, LANES), lambda b, p: (b, 0, p)),
        out_shape=jax.ShapeDtypeStruct((BATCH, SEQ, C_WIDTH), F32),
        scratch_shapes=[
            pltpu.VMEM((SEQ, LANES), BF16),
            pltpu.VMEM((2, LANES, SEQ), BF16),
            pltpu.VMEM((2, SEQ, LANES), BF16),
        ],
        compiler_params=_params("arbitrary", "arbitrary"),
        name="fox",
    )(h3, h3, h3, crow4, gq_tile, gk_tile).reshape(BATCH * SEQ, C_WIDTH)


def _outproj_odd_kernel(y, z, x, w, out):
    g = (y[...] * _silu(z[...])).astype(BF16)
    out[...] = x[...] + _dot(g, w[...])


def _outproj_odd(y, h1, x2d, w_bf16):
    m = x2d.shape[0]
    full = pl.BlockSpec((ROW_TILE, D_MODEL), lambda i: (i, 0))
    return pl.pallas_call(
        _outproj_odd_kernel,
        grid=(m // ROW_TILE,),
        in_specs=[
            full,
            pl.BlockSpec((ROW_TILE, C_WIDTH), lambda i: (i, O_Z // C_WIDTH)),
            full,
            pl.BlockSpec((C_WIDTH, D_MODEL), lambda i: (0, 0)),
        ],
        out_specs=full,
        out_shape=jax.ShapeDtypeStruct((m, D_MODEL), F32),
        compiler_params=_params("arbitrary"),
        name="outproj_odd",
    )(y, h1, x2d, w_bf16)


def _even_weight(w):
    qa, ka, va, za, qb, kb, vb, zb, qi, ki, wi = jnp.split(
        w, np.cumsum([512, 512, 512, 512, 512, 64, 64, 512, 256, 32, 8])[:-1].tolist(), axis=-1)
    pad = jnp.zeros((w.shape[0], E_COLS - (E_KIW + IDX_DIM + IDX_HEADS)), w.dtype)
    return jnp.concatenate([qa, ka, va, za, qb, zb, qi, kb, vb, ki, wi, pad], axis=-1).astype(BF16)


def _pad_lanes(v, width):
    return jnp.pad(v, (0, width - v.shape[0])).reshape(1, width)


def _tile_heads(g, heads):
    return jnp.tile(g, heads).reshape(1, heads * HEAD_DIM)


def kernel(x, even_norm, even_w_in, even_q_norm_a, even_k_norm_a, even_q_norm_b, even_k_norm_b,
           even_k_norm_idx, even_w_out, odd_norm, odd_w_in, odd_b_forget, odd_q_norm, odd_k_norm,
           odd_w_out):
    assert x.shape == (BATCH, SEQ, D_MODEL)
    x2d = x.reshape(BATCH * SEQ, D_MODEL)

    h0 = _inproj(x2d, even_norm[0].reshape(1, D_MODEL), _even_weight(even_w_in[0]), chunk=512)
    slope_tile = jnp.asarray(
        np.repeat([2.0 ** (-8.0 * (i + 1) / A_HEADS) for i in range(A_HEADS)], HEAD_DIM), F32
    ).reshape(1, A_WIDTH)
    gqa = _tile_heads(even_q_norm_a[0], A_HEADS)
    gka = _tile_heads(even_k_norm_a[0], A_HEADS)
    ya = _dilated(h0, gqa, gka, slope_tile)
    yb = _dsa(
        h0,
        _tile_heads(even_q_norm_b[0], B_HEADS),
        _pad_lanes(even_k_norm_b[0], LANES),
        _pad_lanes(even_k_norm_idx[0], LANES),
        buckets=tuple(range(256, SEQ + 1, 256)),
    )
    x1 = _outproj_even(ya, h0, yb, x2d, even_w_out[0].astype(BF16))

    w1 = jnp.pad(odd_w_in[0], ((0, 0), (0, O_COLS - odd_w_in.shape[-1]))).astype(BF16)
    h1 = _inproj(x1, odd_norm[0].reshape(1, D_MODEL), w1, chunk=384)
    crow = _forget_cumsum(h1, _pad_lanes(odd_b_forget[0], LANES))
    y = _fox(h1, crow, _tile_heads(odd_q_norm[0], 2), _tile_heads(odd_k_norm[0], 2))
    out = _outproj_odd(y, h1, x1, odd_w_out[0].astype(BF16))
    return out.reshape(BATCH, SEQ, D_MODEL)
```

```python
import functools

import numpy as np
import jax
import jax.numpy as jnp
from jax import lax
from jax.experimental import pallas as pl
from jax.experimental.pallas import tpu as pltpu

F32 = jnp.float32
BF16 = jnp.bfloat16

D_MODEL = 1024
BATCH = 16
SEQ = 2048
HEAD_DIM = 64
Q_BLOCK = 128
QUARTER = SEQ // 4
EPS = 1e-6
LANES = 128

A_HEADS = 8
A_GROUPS = ((128, 1), (512, 4), (2048, 16))
A_WIDTH = A_HEADS * HEAD_DIM
B_HEADS = 8
B_WIDTH = B_HEADS * HEAD_DIM
IDX_HEADS = 8
IDX_DIM = 32
IDX_TOPK = 256
C_HEADS = 16
C_WIDTH = C_HEADS * HEAD_DIM

E_QA, E_KA, E_VA, E_ZA = 0, 512, 1024, 1536
E_QB, E_ZB, E_QI, E_KV, E_KIW = 2048, 2560, 3072, 3328, 3456
E_COLS = 3584
O_Q, O_K, O_V, O_Z, O_FG = 0, 1024, 2048, 3072, 4096
O_COLS = 4224

ROW_TILE = 512
VMEM_LIMIT = 48 * 1024 * 1024
NEG_INF = float("-inf")
F32_LOWEST = float(np.finfo(np.float32).min)
INT_MIN = -(2 ** 31)
LOG2E = float(np.log2(np.e))
FIRST_PER_ITER = 4
REST_PER_ITER = {1: 5, 4: 4}
FOX_Q_BLOCK = 256
COUNT_SLAB = 64

_NT = (((1,), (1,)), ((), ()))


def _params(*sem):
    return pltpu.CompilerParams(dimension_semantics=sem, vmem_limit_bytes=VMEM_LIMIT)


def _dot(a, b):
    return jnp.dot(a, b, preferred_element_type=F32)


def _dot_nt(a, b):
    return lax.dot_general(a, b, _NT, preferred_element_type=F32)


def _lane_iota(shape):
    return lax.broadcasted_iota(jnp.int32, shape, len(shape) - 1)


def _segment_mean_sq(x, seg):
    r = lax.broadcasted_iota(jnp.int32, (LANES, LANES), 0) // seg
    c = lax.broadcasted_iota(jnp.int32, (LANES, LANES), 1) // seg
    bd = jnp.where(r == c, 1.0 / seg, 0.0).astype(BF16)
    xx = x * x
    hi = xx.astype(BF16)
    lo = (xx - hi.astype(F32)).astype(BF16)
    return _dot(hi, bd) + _dot(lo, bd)


def _head_rms_norm(x, g):
    w = x.shape[-1]
    tiles = []
    for c in range(0, w, LANES):
        xt = x[:, c:c + LANES]
        ms = _segment_mean_sq(xt, HEAD_DIM)
        tiles.append(xt * lax.rsqrt(ms + EPS) * g[:, c:c + LANES])
    return tiles[0] if len(tiles) == 1 else jnp.concatenate(tiles, axis=-1)


def _with_ones_lane(v_placed, lane, head):
    return jnp.where(lane == _ones_lane(head), 1.0, v_placed).astype(BF16)


def _ones_lane(head):
    return HEAD_DIM if head == 0 else 0


def _pair_output(prs, vp_s, s_len, low):
    pv = [_dot(prs[e].astype(BF16), vp_s[e, 0:s_len, :]) for e in range(2)]
    ls = [pv[e][:, _ones_lane(e):_ones_lane(e) + 1] for e in range(2)]
    return jnp.where(low, pv[0] / ls[0], pv[1] / ls[1])


def _silu(z):
    return z / (1.0 + jnp.exp(-z))


def _inproj_kernel(x_ref, g_ref, w_ref, o_ref, *, chunk):
    x = x_ref[...]
    ms = jnp.mean(x * x, axis=-1, keepdims=True)
    xn = (x * lax.rsqrt(ms + EPS) * g_ref[...]).astype(BF16)
    n = o_ref.shape[-1]
    for c in range(0, n, chunk):
        hi = min(c + chunk, n)
        o_ref[:, c:hi] = _dot(xn, w_ref[:, c:hi])


def _inproj(x2d, g, w_bf16, chunk):
    m, d = x2d.shape
    n = w_bf16.shape[1]
    return pl.pallas_call(
        functools.partial(_inproj_kernel, chunk=chunk),
        grid=(m // ROW_TILE,),
        in_specs=[
            pl.BlockSpec((ROW_TILE, d), lambda i: (i, 0)),
            pl.BlockSpec((1, d), lambda i: (0, 0)),
            pl.BlockSpec((d, n), lambda i: (0, 0), pipeline_mode=pl.Buffered(1)),
        ],
        out_specs=pl.BlockSpec((ROW_TILE, n), lambda i: (i, 0)),
        out_shape=jax.ShapeDtypeStruct((m, n), F32),
        compiler_params=_params("arbitrary"),
        name="inproj",
    )(x2d, g, w_bf16)


def _rows(start, size, stride):
    return pl.ds(start, size) if stride == 1 else pl.ds(start, size, stride=stride)


def _dilated_kernel(q_ref, k_ref, v_ref, gq_ref, gk_ref, sl_ref, y_ref,
                    qn_s, kn_s, q4_s, k4_s, v4_s, pv_s, m_s, l_s):
    qn_s[...] = _head_rms_norm(q_ref[...], gq_ref[...]) * (HEAD_DIM ** -0.5 * LOG2E)
    kn_s[...] = _head_rms_norm(k_ref[...], gk_ref[...])
    for c in range(4):
        dst = slice(c * QUARTER, (c + 1) * QUARTER)
        src = _rows(c, QUARTER, 4)
        q4_s[dst, :] = qn_s[src, :]
        k4_s[dst, :] = kn_s[src, :]
        v4_s[dst, :] = v_ref[src, :]

    lane = _lane_iota((1, LANES))
    low = lane < HEAD_DIM
    qi = lax.broadcasted_iota(jnp.int32, (Q_BLOCK, 2 * Q_BLOCK), 0)
    kj = lax.broadcasted_iota(jnp.int32, (Q_BLOCK, 2 * Q_BLOCK), 1)
    dsub = Q_BLOCK + qi - kj
    slopes = sl_ref[...] * LOG2E
    slope_pair = (slopes[:, 0:1], slopes[:, HEAD_DIM:HEAD_DIM + 1])

    for g, (window, dil) in enumerate(A_GROUPS):
        span = window // dil
        nb = SEQ // dil // Q_BLOCK
        base = jnp.where(dsub >= 0, jnp.where(dsub <= span, (-dil * dsub).astype(F32), NEG_INF), NEG_INF)
        biases = (base * slope_pair[0], base * slope_pair[1])
        q_src, k_src, v_src = (qn_s, kn_s, v_ref) if g == 0 else (q4_s, k4_s, v4_s)
        stride = 4 if dil == 16 else 1

        def load(q0, first, stride=stride, q_src=q_src, k_src=k_src, v_src=v_src):
            qr = _rows(q0, Q_BLOCK, stride)
            kr = qr if first else _rows(q0 - stride * Q_BLOCK, 2 * Q_BLOCK, stride)
            return q_src[qr, :], k_src[kr, :].astype(BF16), v_src[kr, :].astype(BF16)

        def compute(loaded, first, biases=biases):
            n = len(loaded)
            lgs = [
                _dot_nt(jnp.where(low if e == 0 else ~low, qb, 0.0).astype(BF16), kk)
                + (biases[e][:, Q_BLOCK:] if first else biases[e])
                for qb, kk, _ in loaded for e in range(2)
            ]
            ms = [jnp.max(lg, axis=-1, keepdims=True) for lg in lgs]
            prs = [jnp.exp2(lg - m) for lg, m in zip(lgs, ms)]
            ls = [jnp.sum(pr, axis=-1, keepdims=True) for pr in prs]
            pvs = [_dot(pr.astype(BF16), loaded[i // 2][2]) for i, pr in enumerate(prs)]
            return [
                (jnp.where(low, pvs[2 * b], pvs[2 * b + 1]), jnp.where(low, ms[2 * b], ms[2 * b + 1]),
                 jnp.where(low, ls[2 * b], ls[2 * b + 1]))
                for b in range(n)
            ]

        def store(q0, res, g=g, stride=stride):
            qr = _rows(q0, Q_BLOCK, stride)
            pv_s[g, qr, :] = res[0]
            m_s[g, qr, :] = res[1]
            l_s[g, qr, :] = res[2]

        def run_blocks(start_of, count, per_iter, first, load=load, compute=compute, store=store):
            def body(t, carry):
                starts = [start_of(t * per_iter + u) for u in range(per_iter)]
                loaded = [load(q0, first) for q0 in starts]
                results = compute(loaded, first)
                for q0, res in zip(starts, results):
                    store(q0, res)
                return carry
            trips = count // per_iter
            if trips == 1:
                body(0, 0)
            else:
                lax.fori_loop(0, trips, body, 0)

        per = nb - 1
        if dil == 1:
            first_start = lambda idx: 0
            rest_start = lambda idx: Q_BLOCK * (1 + idx)
        elif dil == 4:
            first_start = lambda idx: idx * QUARTER
            rest_start = lambda idx, per=per: lax.div(idx, per) * QUARTER + Q_BLOCK * (1 + lax.rem(idx, per))
        else:
            first_start = lambda idx: lax.rem(idx, 4) * QUARTER + lax.div(idx, 4)
            rest_start = None
        run_blocks(first_start, dil, min(dil, FIRST_PER_ITER), True)
        if per:
            run_blocks(rest_start, dil * per, REST_PER_ITER[dil], False)

    chunk = 2 * Q_BLOCK
    for c in range(4):
        for o in range(0, QUARTER, chunk):
            nat = _rows(c + 4 * o, chunk, 4)
            grp = slice(c * QUARTER + o, c * QUARTER + o + chunk)
            m0, m1, m2 = m_s[0, nat, :], m_s[1, grp, :], m_s[2, grp, :]
            mm = jnp.maximum(jnp.maximum(m0, m1), m2)
            w0, w1, w2 = jnp.exp2(m0 - mm), jnp.exp2(m1 - mm), jnp.exp2(m2 - mm)
            num = w0 * pv_s[0, nat, :] + w1 * pv_s[1, grp, :] + w2 * pv_s[2, grp, :]
            den = w0 * l_s[0, nat, :] + w1 * l_s[1, grp, :] + w2 * l_s[2, grp, :]
            y_ref[nat, :] = num / den


def _dilated(h0, gq_tile, gk_tile, slope_tile):
    h3 = h0.reshape(BATCH, SEQ, E_COLS)
    blk = lambda off: pl.BlockSpec((None, SEQ, LANES), lambda b, p: (b, 0, off // LANES + p))
    par = pl.BlockSpec((1, LANES), lambda b, p: (0, p))
    seq_tile = pltpu.VMEM((SEQ, LANES), F32)
    grp_tile = pltpu.VMEM((len(A_GROUPS), SEQ, LANES), F32)
    return pl.pallas_call(
        _dilated_kernel,
        grid=(BATCH, A_WIDTH // LANES),
        in_specs=[blk(E_QA), blk(E_KA), blk(E_VA), par, par, par],
        out_specs=pl.BlockSpec((None, SEQ, LANES), lambda b, p: (b, 0, p)),
        out_shape=jax.ShapeDtypeStruct((BATCH, SEQ, A_WIDTH), F32),
        scratch_shapes=[seq_tile] * 5 + [grp_tile] * 3,
        compiler_params=_params("arbitrary", "arbitrary"),
        name="dilated",
    )(h3, h3, h3, gq_tile, gk_tile, slope_tile).reshape(BATCH * SEQ, A_WIDTH)


def _count_ge(score_ref, s_len, p):
    x = score_ref[:, 0:s_len]
    return jnp.sum(jnp.where(x >= p, 1.0, 0.0), axis=-1, keepdims=True)


def _key_to_float(key):
    bits = key ^ ((key >> 31) & 0x7FFFFFFF)
    return lax.bitcast_convert_type(bits, F32)


def _dsa_kernel(qb_ref, kv_ref, qi_ref, kiw_ref, wq_ref, gq_ref, gk_ref, gi_ref, o_ref,
                ki_t_s, kb_t_s, v_s, score_s, score_t_s, bias_s, *, buckets):
    i = pl.program_id(1)
    lane = _lane_iota((1, LANES))
    kf = float(IDX_TOPK)

    @pl.when(i == 0)
    def _prep():
        kiw = kiw_ref[...]
        ki = jnp.where(lane < IDX_DIM, kiw, 0.0)
        ms = jnp.sum(ki * ki, axis=-1, keepdims=True) * (1.0 / IDX_DIM)
        kin = ki * lax.rsqrt(ms + EPS) * gi_ref[...]
        kv = kv_ref[...]
        kb = jnp.where(lane < HEAD_DIM, kv, 0.0)
        ms = jnp.sum(kb * kb, axis=-1, keepdims=True) * (1.0 / HEAD_DIM)
        kbn = kb * lax.rsqrt(ms + EPS) * gk_ref[...]
        for c0 in range(0, SEQ, LANES):
            ki_t_s[:, c0:c0 + LANES] = kin[c0:c0 + LANES, :].T[0:IDX_DIM, :].astype(BF16)
            kb_t_s[:, c0:c0 + LANES] = kbn[c0:c0 + LANES, :].T[0:HEAD_DIM, :].astype(BF16)
        v_s[...] = _with_ones_lane(jnp.where(lane >= HEAD_DIM, kv, 0.0), lane, 1)

    qn = _head_rms_norm(qb_ref[...], gq_ref[...]) * (HEAD_DIM ** -0.5 * LOG2E)
    q_heads = [qn[:, h * HEAD_DIM:(h + 1) * HEAD_DIM] for h in range(B_HEADS)]
    qraw = qi_ref[...]
    qi_all = jnp.concatenate(
        [qraw[:, h * IDX_DIM:(h + 1) * IDX_DIM] for h in range(IDX_HEADS)], axis=0).astype(BF16)
    wq = wq_ref[...] * (IDX_HEADS ** -0.5 * IDX_DIM ** -0.5)
    t_idx = i * Q_BLOCK + lax.broadcasted_iota(jnp.int32, (Q_BLOCK, 1), 0)

    def body(s_len):
        s_idx = lax.broadcasted_iota(jnp.int32, (1, s_len), 1)
        causal = s_idx <= t_idx

        sh_all = _dot(qi_all, ki_t_s[:, 0:s_len])
        score = jnp.zeros((Q_BLOCK, s_len), F32)
        for h in range(IDX_HEADS):
            sh = sh_all[h * Q_BLOCK:(h + 1) * Q_BLOCK, :]
            score = score + wq[:, IDX_DIM + h:IDX_DIM + h + 1] * jnp.maximum(sh, 0.0)
        masked = jnp.where(causal, score, NEG_INF)
        score_s[:, 0:s_len] = masked
        for c0 in range(0, s_len, LANES):
            score_t_s[c0:c0 + LANES, :] = masked[:, c0:c0 + LANES].T

        def count_t(p_row):
            acc = jnp.zeros((COUNT_SLAB, Q_BLOCK), F32)
            for r0 in range(0, s_len, COUNT_SLAB):
                acc = acc + jnp.where(score_t_s[r0:r0 + COUNT_SLAB, :] >= p_row, 1.0, 0.0)
            return jnp.sum(acc, axis=0, keepdims=True)

        def step(it, tu):
            bit = lax.shift_left(jnp.int32(1), 31 - it)
            cand = tu | bit
            p = _key_to_float(cand ^ INT_MIN)
            return jnp.where(count_t(p) >= kf, cand, tu)

        tu = lax.fori_loop(0, 32, step, jnp.zeros((1, Q_BLOCK), jnp.int32))
        key = tu ^ INT_MIN
        short_row = (i * Q_BLOCK + lane + 1).astype(F32) <= kf
        lo_row = jnp.where(short_row, F32_LOWEST, _key_to_float(key))
        hi_row = _key_to_float(key + 1)
        cnt_lo_row = jnp.where(short_row, kf, count_t(lo_row))
        eye = (lax.broadcasted_iota(jnp.int32, (Q_BLOCK, Q_BLOCK), 0)
               == lax.broadcasted_iota(jnp.int32, (Q_BLOCK, Q_BLOCK), 1))

        def to_col(row):
            return jnp.sum(jnp.where(eye, row, 0.0), axis=-1, keepdims=True)

        lo = to_col(lo_row)
        bias_s[:, 0:s_len] = jnp.where(score_s[:, 0:s_len] >= lo, 0.0, NEG_INF)

        unresolved = jnp.max(jnp.where(cnt_lo_row != kf, 1.0, 0.0)) > 0.0

        @pl.when(unresolved)
        def _ties():
            hi = to_col(hi_row)
            cnt_lo = to_col(cnt_lo_row)

            def active(lo_, hi_, c_):
                mid = lo_ + (hi_ - lo_) * 0.5
                act = jnp.where(c_ != kf, jnp.where(mid > lo_, jnp.where(mid < hi_, 1.0, 0.0), 0.0), 0.0)
                return mid, act

            def cond(c):
                return c[4] > 0

            def wbody(c):
                lo_, hi_, c_, it, _ = c
                mid, act = active(lo_, hi_, c_)
                cnt = _count_ge(score_s, s_len, mid)
                up = jnp.where(cnt >= kf, act, 0.0) > 0.0
                dn = jnp.where(cnt >= kf, 0.0, act) > 0.0
                lo2 = jnp.where(up, mid, lo_)
                c2 = jnp.where(up, cnt, c_)
                hi2 = jnp.where(dn, mid, hi_)
                _, act2 = active(lo2, hi2, c2)
                go = jnp.where(jnp.max(act2) > 0.0, 1, 0) * jnp.where(it < 200, 1, 0)
                return lo2, hi2, c2, it + 1, go.astype(jnp.int32)

            _, act0 = active(lo, hi, cnt_lo)
            go0 = jnp.where(jnp.max(act0) > 0.0, 1, 0).astype(jnp.int32)
            lo_f, _, _, _, _ = lax.while_loop(cond, wbody, (lo, hi, cnt_lo, jnp.int32(0), go0))

            sc = score_s[:, 0:s_len]
            gt = sc > lo_f
            need = kf - jnp.sum(jnp.where(gt, 1.0, 0.0), axis=-1, keepdims=True)
            eq = jnp.where(sc == lo_f, 1.0, 0.0)
            rr = lax.broadcasted_iota(jnp.int32, (LANES, LANES), 0)
            cc = lax.broadcasted_iota(jnp.int32, (LANES, LANES), 1)
            upper = jnp.where(rr < cc, 1.0, 0.0).astype(BF16)
            carry = jnp.zeros((Q_BLOCK, 1), F32)
            for c0 in range(0, s_len, LANES):
                eq_c = eq[:, c0:c0 + LANES]
                prefix = _dot(eq_c.astype(BF16), upper) + carry
                keep = jnp.where(gt[:, c0:c0 + LANES], 1.0, jnp.where(prefix < need, eq_c, 0.0))
                bias_s[:, c0:c0 + LANES] = jnp.where(keep > 0.0, 0.0, NEG_INF)
                carry = carry + jnp.sum(eq_c, axis=-1, keepdims=True)

        key_pos = (s_idx - i * Q_BLOCK).astype(F32)
        low = lane < HEAD_DIM
        for pr_i in range(B_HEADS // 2):
            q2 = jnp.concatenate(q_heads[2 * pr_i:2 * pr_i + 2], axis=0).astype(BF16)
            qk = _dot(q2, kb_t_s[:, 0:s_len])
            slopes = [2.0 ** (-8.0 * (2 * pr_i + e + 1) / B_HEADS) * LOG2E for e in range(2)]
            lgs = [qk[e * Q_BLOCK:(e + 1) * Q_BLOCK, :] + slopes[e] * key_pos + bias_s[:, 0:s_len]
                   for e in range(2)]
            ms = [jnp.max(lg, axis=-1, keepdims=True) for lg in lgs]
            prs = [jnp.exp2(lg - m).astype(BF16) for lg, m in zip(lgs, ms)]
            pv = _dot(jnp.concatenate(prs, axis=0), v_s[0:s_len, :])
            outs = [pv[e * Q_BLOCK:(e + 1) * Q_BLOCK, :] for e in range(2)]
            outs = [o / o[:, 0:1] for o in outs]
            o_ref[:, pr_i * LANES:(pr_i + 1) * LANES] = jnp.where(
                low, pltpu.roll(outs[0], HEAD_DIM, axis=1), outs[1])

    per = (SEQ // Q_BLOCK) // len(buckets)
    for bi, s_len in enumerate(buckets):
        pl.when(i // per == bi)(functools.partial(body, s_len))


def _dsa(h0, gq_tile, gk_pad, gi_pad, buckets):
    h3 = h0.reshape(BATCH, SEQ, E_COLS)
    nq = SEQ // Q_BLOCK
    return pl.pallas_call(
        functools.partial(_dsa_kernel, buckets=buckets),
        grid=(BATCH, nq),
        in_specs=[
            pl.BlockSpec((None, Q_BLOCK, B_WIDTH), lambda b, i: (b, i, E_QB // B_WIDTH)),
            pl.BlockSpec((None, SEQ, LANES), lambda b, i: (b, 0, E_KV // LANES)),
            pl.BlockSpec((None, Q_BLOCK, 2 * LANES), lambda b, i: (b, i, E_QI // (2 * LANES))),
            pl.BlockSpec((None, SEQ, LANES), lambda b, i: (b, 0, E_KIW // LANES)),
            pl.BlockSpec((None, Q_BLOCK, LANES), lambda b, i: (b, i, E_KIW // LANES)),
            pl.BlockSpec((1, B_WIDTH), lambda b, i: (0, 0)),
            pl.BlockSpec((1, LANES), lambda b, i: (0, 0)),
            pl.BlockSpec((1, LANES), lambda b, i: (0, 0)),
        ],
        out_specs=pl.BlockSpec((None, Q_BLOCK, B_WIDTH), lambda b, i: (b, i, 0)),
        out_shape=jax.ShapeDtypeStruct((BATCH, SEQ, B_WIDTH), F32),
        scratch_shapes=[
            pltpu.VMEM((IDX_DIM, SEQ), BF16),
            pltpu.VMEM((HEAD_DIM, SEQ), BF16),
            pltpu.VMEM((SEQ, LANES), BF16),
            pltpu.VMEM((Q_BLOCK, SEQ), F32),
            pltpu.VMEM((SEQ, Q_BLOCK), F32),
            pltpu.VMEM((Q_BLOCK, SEQ), F32),
        ],
        compiler_params=_params("arbitrary", "arbitrary"),
        name="dsa",
    )(h3, h3, h3, h3, h3, gq_tile, gk_pad, gi_pad).reshape(BATCH * SEQ, B_WIDTH)


def _outproj_even_kernel(ya, za, yb, zb, x, w, out):
    ga = (ya[...] * _silu(za[...])).astype(BF16)
    gb = (yb[...] * _silu(zb[...])).astype(BF16)
    out[...] = x[...] + _dot(ga, w[0:A_WIDTH, :]) + _dot(gb, w[A_WIDTH:, :])


def _outproj_even(ya, h0, yb, x2d, w_bf16):
    m = x2d.shape[0]
    half = pl.BlockSpec((ROW_TILE, A_WIDTH), lambda i: (i, 0))
    return pl.pallas_call(
        _outproj_even_kernel,
        grid=(m // ROW_TILE,),
        in_specs=[
            half,
            pl.BlockSpec((ROW_TILE, A_WIDTH), lambda i: (i, E_ZA // A_WIDTH)),
            half,
            pl.BlockSpec((ROW_TILE, B_WIDTH), lambda i: (i, E_ZB // B_WIDTH)),
            pl.BlockSpec((ROW_TILE, D_MODEL), lambda i: (i, 0)),
            pl.BlockSpec((A_WIDTH + B_WIDTH, D_MODEL), lambda i: (0, 0)),
        ],
        out_specs=pl.BlockSpec((ROW_TILE, D_MODEL), lambda i: (i, 0)),
        out_shape=jax.ShapeDtypeStruct((m, D_MODEL), F32),
        compiler_params=_params("arbitrary"),
        name="outproj_even",
    )(ya, h0, yb, h0, x2d, w_bf16)


def _forget_kernel(fg_ref, b_ref, crow_ref):
    x = fg_ref[...] + b_ref[...]
    logf = jnp.minimum(x, 0.0) - jnp.log1p(jnp.exp(-jnp.abs(x)))
    rr = lax.broadcasted_iota(jnp.int32, (LANES, LANES), 0)
    cc = lax.broadcasted_iota(jnp.int32, (LANES, LANES), 1)
    tri = jnp.where(cc <= rr, 1.0, 0.0).astype(BF16)
    carry = jnp.zeros((1, LANES), F32)
    for blk in range(SEQ // LANES):
        rows = slice(blk * LANES, (blk + 1) * LANES)
        v = logf[rows, :]
        t0 = v.astype(BF16)
        r1 = v - t0.astype(F32)
        t1 = r1.astype(BF16)
        t2 = (r1 - t1.astype(F32)).astype(BF16)
        c = _dot(tri, t0) + _dot(tri, t1) + _dot(tri, t2) + carry
        crow_ref[:, rows] = c.T[0:C_HEADS, :]
        carry = c[LANES - 1:LANES, :]


def _forget_cumsum(h1, b_pad):
    h3 = h1.reshape(BATCH, SEQ, O_COLS)
    return pl.pallas_call(
        _forget_kernel,
        grid=(BATCH,),
        in_specs=[
            pl.BlockSpec((None, SEQ, LANES), lambda b: (b, 0, O_FG // LANES)),
            pl.BlockSpec((1, LANES), lambda b: (0, 0)),
        ],
        out_specs=pl.BlockSpec((None, C_HEADS, SEQ), lambda b: (b, 0, 0)),
        out_shape=jax.ShapeDtypeStruct((BATCH, C_HEADS, SEQ), F32),
        compiler_params=_params("arbitrary"),
        name="forget_cumsum",
    )(h3, b_pad)


def _fox_kernel(q_ref, k_ref, v_ref, crow_ref, gq_ref, gk_ref, y_ref, qn_s, kp_t_s, vp_s):
    lane = _lane_iota((1, LANES))
    low = lane < HEAD_DIM
    qn_s[...] = (_head_rms_norm(q_ref[...], gq_ref[...]) * (HEAD_DIM ** -0.5 * LOG2E)).astype(BF16)
    kn = _head_rms_norm(k_ref[...], gk_ref[...])
    v = v_ref[...]
    for c0 in range(0, SEQ, LANES):
        kc = kn[c0:c0 + LANES, :]
        kp_t_s[0, :, c0:c0 + LANES] = jnp.where(low, kc, 0.0).T.astype(BF16)
        kp_t_s[1, :, c0:c0 + LANES] = jnp.where(low, 0.0, kc).T.astype(BF16)
    vp_s[0] = _with_ones_lane(jnp.where(low, v, 0.0), lane, 0)
    vp_s[1] = _with_ones_lane(jnp.where(low, 0.0, v), lane, 1)

    crow = crow_ref[...]
    on_or_below = (lax.broadcasted_iota(jnp.int32, (FOX_Q_BLOCK, FOX_Q_BLOCK), 1)
                   <= lax.broadcasted_iota(jnp.int32, (FOX_Q_BLOCK, FOX_Q_BLOCK), 0))
    for iq in range(SEQ // FOX_Q_BLOCK):
        rows = slice(iq * FOX_Q_BLOCK, (iq + 1) * FOX_Q_BLOCK)
        s_len = (iq + 1) * FOX_Q_BLOCK
        d0 = s_len - FOX_Q_BLOCK
        qblk = qn_s[rows, :]
        lgs = []
        for e in range(2):
            decay = (crow[e:e + 1, d0:d0 + 1] - crow[e:e + 1, 0:s_len]) * LOG2E
            lg = _dot(qblk, kp_t_s[e, :, 0:s_len]) + decay
            tail = jnp.where(on_or_below, lg[:, d0:], NEG_INF)
            lgs.append(tail if d0 == 0 else jnp.concatenate([lg[:, :d0], tail], axis=-1))
        ms = [jnp.max(lg, axis=-1, keepdims=True) for lg in lgs]
        prs = [jnp.exp2(lg - m) for lg, m in zip(lgs, ms)]
        y_ref[rows, :] = _pair_output(prs, vp_s, s_len, low)


def _fox(h1, crow, gq_tile, gk_tile):
    h3 = h1.reshape(BATCH, SEQ, O_COLS)
    crow4 = crow.reshape(BATCH, C_HEADS // 2, 2, SEQ)
    blk = lambda off: pl.BlockSpec((None, SEQ, LANES), lambda b, p: (b, 0, off // LANES + p))
    par = pl.BlockSpec((1, LANES), lambda b, p: (0, 0))
    return pl.pallas_call(
        _fox_kernel,
        grid=(BATCH, C_HEADS // 2),
        in_specs=[
            blk(O_Q), blk(O_K), blk(O_V),
            pl.BlockSpec((None, None, 2, SEQ), lambda b, p: (b, p, 0, 0)),
            par, par,
        ],
        out_specs=pl.BlockSpec((None, SEQ, LANES), lambda b, p: (b, 0, p)),
        out_shape=jax.ShapeDtypeStruct((BATCH, SEQ, C_WIDTH), F32),
        scratch_shapes=[
            pltpu.VMEM((SEQ, LANES), BF16),
            pltpu.VMEM((2, LANES, SEQ), BF16),
            pltpu.VMEM((2, SEQ, LANES), BF16),
        ],
        compiler_params=_params("arbitrary", "arbitrary"),
        name="fox",
    )(h3, h3, h3, crow4, gq_tile, gk_tile).reshape(BATCH * SEQ, C_WIDTH)


def _outproj_odd_kernel(y, z, x, w, out):
    g = (y[...] * _silu(z[...])).astype(BF16)
    out[...] = x[...] + _dot(g, w[...])


def _outproj_odd(y, h1, x2d, w_bf16):
    m = x2d.shape[0]
    full = pl.BlockSpec((ROW_TILE, D_MODEL), lambda i: (i, 0))
    return pl.pallas_call(
        _outproj_odd_kernel,
        grid=(m // ROW_TILE,),
        in_specs=[
            full,
            pl.BlockSpec((ROW_TILE, C_WIDTH), lambda i: (i, O_Z // C_WIDTH)),
            full,
            pl.BlockSpec((C_WIDTH, D_MODEL), lambda i: (0, 0)),
        ],
        out_specs=full,
        out_shape=jax.ShapeDtypeStruct((m, D_MODEL), F32),
        compiler_params=_params("arbitrary"),
        name="outproj_odd",
    )(y, h1, x2d, w_bf16)


def _even_weight(w):
    qa, ka, va, za, qb, kb, vb, zb, qi, ki, wi = jnp.split(
        w, np.cumsum([512, 512, 512, 512, 512, 64, 64, 512, 256, 32, 8])[:-1].tolist(), axis=-1)
    pad = jnp.zeros((w.shape[0], E_COLS - (E_KIW + IDX_DIM + IDX_HEADS)), w.dtype)
    return jnp.concatenate([qa, ka, va, za, qb, zb, qi, kb, vb, ki, wi, pad], axis=-1).astype(BF16)


def _pad_lanes(v, width):
    return jnp.pad(v, (0, width - v.shape[0])).reshape(1, width)


def _tile_heads(g, heads):
    return jnp.tile(g, heads).reshape(1, heads * HEAD_DIM)


def kernel(x, even_norm, even_w_in, even_q_norm_a, even_k_norm_a, even_q_norm_b, even_k_norm_b,
           even_k_norm_idx, even_w_out, odd_norm, odd_w_in, odd_b_forget, odd_q_norm, odd_k_norm,
           odd_w_out):
    assert x.shape == (BATCH, SEQ, D_MODEL)
    x2d = x.reshape(BATCH * SEQ, D_MODEL)

    h0 = _inproj(x2d, even_norm[0].reshape(1, D_MODEL), _even_weight(even_w_in[0]), chunk=512)
    slope_tile = jnp.asarray(
        np.repeat([2.0 ** (-8.0 * (i + 1) / A_HEADS) for i in range(A_HEADS)], HEAD_DIM), F32
    ).reshape(1, A_WIDTH)
    gqa = _tile_heads(even_q_norm_a[0], A_HEADS)
    gka = _tile_heads(even_k_norm_a[0], A_HEADS)
    ya = _dilated(h0, gqa, gka, slope_tile)
    yb = _dsa(
        h0,
        _tile_heads(even_q_norm_b[0], B_HEADS),
        _pad_lanes(even_k_norm_b[0], LANES),
        _pad_lanes(even_k_norm_idx[0], LANES),
        buckets=tuple(range(256, SEQ + 1, 256)),
    )
    x1 = _outproj_even(ya, h0, yb, x2d, even_w_out[0].astype(BF16))

    w1 = jnp.pad(odd_w_in[0], ((0, 0), (0, O_COLS - odd_w_in.shape[-1]))).astype(BF16)
    h1 = _inproj(x1, odd_norm[0].reshape(1, D_MODEL), w1, chunk=512)
    crow = _forget_cumsum(h1, _pad_lanes(odd_b_forget[0], LANES))
    y = _fox(h1, crow, _tile_heads(odd_q_norm[0], 2), _tile_heads(odd_k_norm[0], 2))
    out = _outproj_odd(y, h1, x1, odd_w_out[0].astype(BF16))
    return out.reshape(BATCH, SEQ, D_MODEL)
```

```python
import functools

import numpy as np
import jax
import jax.numpy as jnp
from jax import lax
from jax.experimental import pallas as pl
from jax.experimental.pallas import tpu as pltpu

F32 = jnp.float32
BF16 = jnp.bfloat16

D_MODEL = 1024
BATCH = 16
SEQ = 2048
HEAD_DIM = 64
Q_BLOCK = 128
QUARTER = SEQ // 4
EPS = 1e-6
LANES = 128

A_HEADS = 8
A_GROUPS = ((128, 1), (512, 4), (2048, 16))
A_WIDTH = A_HEADS * HEAD_DIM
B_HEADS = 8
B_WIDTH = B_HEADS * HEAD_DIM
IDX_HEADS = 8
IDX_DIM = 32
IDX_TOPK = 256
C_HEADS = 16
C_WIDTH = C_HEADS * HEAD_DIM

E_QA, E_KA, E_VA, E_ZA = 0, 512, 1024, 1536
E_QB, E_ZB, E_QI, E_KV, E_KIW = 2048, 2560, 3072, 3328, 3456
E_COLS = 3584
O_Q, O_K, O_V, O_Z, O_FG = 0, 1024, 2048, 3072, 4096
O_COLS = 4224

ROW_TILE = 512
VMEM_LIMIT = 48 * 1024 * 1024
NEG_INF = float("-inf")
F32_LOWEST = float(np.finfo(np.float32).min)
INT_MIN = -(2 ** 31)
LOG2E = float(np.log2(np.e))
FIRST_PER_ITER = 4
REST_PER_ITER = {1: 5, 4: 4}
FOX_Q_BLOCK = 256
COUNT_SLAB = 64

_NT = (((1,), (1,)), ((), ()))


def _params(*sem):
    return pltpu.CompilerParams(dimension_semantics=sem, vmem_limit_bytes=VMEM_LIMIT)


def _dot(a, b):
    return jnp.dot(a, b, preferred_element_type=F32)


def _dot_nt(a, b):
    return lax.dot_general(a, b, _NT, preferred_element_type=F32)


def _lane_iota(shape):
    return lax.broadcasted_iota(jnp.int32, shape, len(shape) - 1)


def _segment_mean_sq(x, seg):
    r = lax.broadcasted_iota(jnp.int32, (LANES, LANES), 0) // seg
    c = lax.broadcasted_iota(jnp.int32, (LANES, LANES), 1) // seg
    bd = jnp.where(r == c, 1.0 / seg, 0.0).astype(BF16)
    xx = x * x
    hi = xx.astype(BF16)
    lo = (xx - hi.astype(F32)).astype(BF16)
    return _dot(hi, bd) + _dot(lo, bd)


def _head_rms_norm(x, g):
    w = x.shape[-1]
    tiles = []
    for c in range(0, w, LANES):
        xt = x[:, c:c + LANES]
        ms = _segment_mean_sq(xt, HEAD_DIM)
        tiles.append(xt * lax.rsqrt(ms + EPS) * g[:, c:c + LANES])
    return tiles[0] if len(tiles) == 1 else jnp.concatenate(tiles, axis=-1)


def _with_ones_lane(v_placed, lane, head):
    return jnp.where(lane == _ones_lane(head), 1.0, v_placed).astype(BF16)


def _ones_lane(head):
    return HEAD_DIM if head == 0 else 0


def _pair_output(prs, vp_s, s_len, low):
    pv = [_dot(prs[e].astype(BF16), vp_s[e, 0:s_len, :]) for e in range(2)]
    ls = [pv[e][:, _ones_lane(e):_ones_lane(e) + 1] for e in range(2)]
    return jnp.where(low, pv[0] / ls[0], pv[1] / ls[1])


def _silu(z):
    return z / (1.0 + jnp.exp(-z))


def _inproj_kernel(x_ref, g_ref, w_ref, o_ref, *, chunk):
    x = x_ref[...]
    ms = jnp.mean(x * x, axis=-1, keepdims=True)
    xn = (x * lax.rsqrt(ms + EPS) * g_ref[...]).astype(BF16)
    n = o_ref.shape[-1]
    for c in range(0, n, chunk):
        hi = min(c + chunk, n)
        o_ref[:, c:hi] = _dot(xn, w_ref[:, c:hi])


def _inproj(x2d, g, w_bf16, chunk):
    m, d = x2d.shape
    n = w_bf16.shape[1]
    return pl.pallas_call(
        functools.partial(_inproj_kernel, chunk=chunk),
        grid=(m // ROW_TILE,),
        in_specs=[
            pl.BlockSpec((ROW_TILE, d), lambda i: (i, 0)),
            pl.BlockSpec((1, d), lambda i: (0, 0)),
            pl.BlockSpec((d, n), lambda i: (0, 0), pipeline_mode=pl.Buffered(1)),
        ],
        out_specs=pl.BlockSpec((ROW_TILE, n), lambda i: (i, 0)),
        out_shape=jax.ShapeDtypeStruct((m, n), F32),
        compiler_params=_params("arbitrary"),
        name="inproj",
    )(x2d, g, w_bf16)


def _rows(start, size, stride):
    return pl.ds(start, size) if stride == 1 else pl.ds(start, size, stride=stride)


def _dilated_kernel(q_ref, k_ref, v_ref, gq_ref, gk_ref, sl_ref, y_ref,
                    qn_s, kn_s, q4_s, k4_s, v4_s, pv_s, m_s, l_s):
    qn_s[...] = _head_rms_norm(q_ref[...], gq_ref[...]) * (HEAD_DIM ** -0.5 * LOG2E)
    kn_s[...] = _head_rms_norm(k_ref[...], gk_ref[...])
    for c in range(4):
        dst = slice(c * QUARTER, (c + 1) * QUARTER)
        src = _rows(c, QUARTER, 4)
        q4_s[dst, :] = qn_s[src, :]
        k4_s[dst, :] = kn_s[src, :]
        v4_s[dst, :] = v_ref[src, :]

    lane = _lane_iota((1, LANES))
    low = lane < HEAD_DIM
    qi = lax.broadcasted_iota(jnp.int32, (Q_BLOCK, 2 * Q_BLOCK), 0)
    kj = lax.broadcasted_iota(jnp.int32, (Q_BLOCK, 2 * Q_BLOCK), 1)
    dsub = Q_BLOCK + qi - kj
    slopes = sl_ref[...] * LOG2E
    slope_pair = (slopes[:, 0:1], slopes[:, HEAD_DIM:HEAD_DIM + 1])

    for g, (window, dil) in enumerate(A_GROUPS):
        span = window // dil
        nb = SEQ // dil // Q_BLOCK
        base = jnp.where(dsub >= 0, jnp.where(dsub <= span, (-dil * dsub).astype(F32), NEG_INF), NEG_INF)
        biases = (base * slope_pair[0], base * slope_pair[1])
        q_src, k_src, v_src = (qn_s, kn_s, v_ref) if g == 0 else (q4_s, k4_s, v4_s)
        stride = 4 if dil == 16 else 1

        def load(q0, first, stride=stride, q_src=q_src, k_src=k_src, v_src=v_src):
            qr = _rows(q0, Q_BLOCK, stride)
            kr = qr if first else _rows(q0 - stride * Q_BLOCK, 2 * Q_BLOCK, stride)
            return q_src[qr, :], k_src[kr, :].astype(BF16), v_src[kr, :].astype(BF16)

        def compute(loaded, first, biases=biases):
            n = len(loaded)
            lgs = [
                _dot_nt(jnp.where(low if e == 0 else ~low, qb, 0.0).astype(BF16), kk)
                + (biases[e][:, Q_BLOCK:] if first else biases[e])
                for qb, kk, _ in loaded for e in range(2)
            ]
            ms = [jnp.max(lg, axis=-1, keepdims=True) for lg in lgs]
            prs = [jnp.exp2(lg - m) for lg, m in zip(lgs, ms)]
            ls = [jnp.sum(pr, axis=-1, keepdims=True) for pr in prs]
            pvs = [_dot(pr.astype(BF16), loaded[i // 2][2]) for i, pr in enumerate(prs)]
            return [
                (jnp.where(low, pvs[2 * b], pvs[2 * b + 1]), jnp.where(low, ms[2 * b], ms[2 * b + 1]),
                 jnp.where(low, ls[2 * b], ls[2 * b + 1]))
                for b in range(n)
            ]

        def store(q0, res, g=g, stride=stride):
            qr = _rows(q0, Q_BLOCK, stride)
            pv_s[g, qr, :] = res[0]
            m_s[g, qr, :] = res[1]
            l_s[g, qr, :] = res[2]

        def run_blocks(start_of, count, per_iter, first, load=load, compute=compute, store=store):
            def body(t, carry):
                starts = [start_of(t * per_iter + u) for u in range(per_iter)]
                loaded = [load(q0, first) for q0 in starts]
                results = compute(loaded, first)
                for q0, res in zip(starts, results):
                    store(q0, res)
                return carry
            trips = count // per_iter
            if trips == 1:
                body(0, 0)
            else:
                lax.fori_loop(0, trips, body, 0)

        per = nb - 1
        if dil == 1:
            first_start = lambda idx: 0
            rest_start = lambda idx: Q_BLOCK * (1 + idx)
        elif dil == 4:
            first_start = lambda idx: idx * QUARTER
            rest_start = lambda idx, per=per: lax.div(idx, per) * QUARTER + Q_BLOCK * (1 + lax.rem(idx, per))
        else:
            first_start = lambda idx: lax.rem(idx, 4) * QUARTER + lax.div(idx, 4)
            rest_start = None
        run_blocks(first_start, dil, min(dil, FIRST_PER_ITER), True)
        if per:
            run_blocks(rest_start, dil * per, REST_PER_ITER[dil], False)

    chunk = 2 * Q_BLOCK
    for c in range(4):
        for o in range(0, QUARTER, chunk):
            nat = _rows(c + 4 * o, chunk, 4)
            grp = slice(c * QUARTER + o, c * QUARTER + o + chunk)
            m0, m1, m2 = m_s[0, nat, :], m_s[1, grp, :], m_s[2, grp, :]
            mm = jnp.maximum(jnp.maximum(m0, m1), m2)
            w0, w1, w2 = jnp.exp2(m0 - mm), jnp.exp2(m1 - mm), jnp.exp2(m2 - mm)
            num = w0 * pv_s[0, nat, :] + w1 * pv_s[1, grp, :] + w2 * pv_s[2, grp, :]
            den = w0 * l_s[0, nat, :] + w1 * l_s[1, grp, :] + w2 * l_s[2, grp, :]
            y_ref[nat, :] = num / den


def _dilated(h0, gq_tile, gk_tile, slope_tile):
    h3 = h0.reshape(BATCH, SEQ, E_COLS)
    blk = lambda off: pl.BlockSpec((None, SEQ, LANES), lambda b, p: (b, 0, off // LANES + p))
    par = pl.BlockSpec((1, LANES), lambda b, p: (0, p))
    seq_tile = pltpu.VMEM((SEQ, LANES), F32)
    grp_tile = pltpu.VMEM((len(A_GROUPS), SEQ, LANES), F32)
    return pl.pallas_call(
        _dilated_kernel,
        grid=(BATCH, A_WIDTH // LANES),
        in_specs=[blk(E_QA), blk(E_KA), blk(E_VA), par, par, par],
        out_specs=pl.BlockSpec((None, SEQ, LANES), lambda b, p: (b, 0, p)),
        out_shape=jax.ShapeDtypeStruct((BATCH, SEQ, A_WIDTH), F32),
        scratch_shapes=[seq_tile] * 5 + [grp_tile] * 3,
        compiler_params=_params("arbitrary", "arbitrary"),
        name="dilated",
    )(h3, h3, h3, gq_tile, gk_tile, slope_tile).reshape(BATCH * SEQ, A_WIDTH)


def _count_ge(score_ref, s_len, p):
    x = score_ref[:, 0:s_len]
    return jnp.sum(jnp.where(x >= p, 1.0, 0.0), axis=-1, keepdims=True)


def _key_to_float(key):
    bits = key ^ ((key >> 31) & 0x7FFFFFFF)
    return lax.bitcast_convert_type(bits, F32)


def _dsa_kernel(qb_ref, kv_ref, qi_ref, kiw_ref, wq_ref, gq_ref, gk_ref, gi_ref, o_ref,
                ki_t_s, kb_t_s, v_s, score_s, score_t_s, bias_s, *, buckets):
    i = pl.program_id(1)
    lane = _lane_iota((1, LANES))
    kf = float(IDX_TOPK)

    @pl.when(i == 0)
    def _prep():
        kiw = kiw_ref[...]
        ki = jnp.where(lane < IDX_DIM, kiw, 0.0)
        ms = jnp.sum(ki * ki, axis=-1, keepdims=True) * (1.0 / IDX_DIM)
        kin = ki * lax.rsqrt(ms + EPS) * gi_ref[...]
        kv = kv_ref[...]
        kb = jnp.where(lane < HEAD_DIM, kv, 0.0)
        ms = jnp.sum(kb * kb, axis=-1, keepdims=True) * (1.0 / HEAD_DIM)
        kbn = kb * lax.rsqrt(ms + EPS) * gk_ref[...]
        for c0 in range(0, SEQ, LANES):
            ki_t_s[:, c0:c0 + LANES] = kin[c0:c0 + LANES, :].T[0:IDX_DIM, :].astype(BF16)
            kb_t_s[:, c0:c0 + LANES] = kbn[c0:c0 + LANES, :].T[0:HEAD_DIM, :].astype(BF16)
        v_s[...] = _with_ones_lane(jnp.where(lane >= HEAD_DIM, kv, 0.0), lane, 1)

    qn = _head_rms_norm(qb_ref[...], gq_ref[...]) * (HEAD_DIM ** -0.5 * LOG2E)
    q_heads = [qn[:, h * HEAD_DIM:(h + 1) * HEAD_DIM] for h in range(B_HEADS)]
    qraw = qi_ref[...]
    qi_all = jnp.concatenate(
        [qraw[:, h * IDX_DIM:(h + 1) * IDX_DIM] for h in range(IDX_HEADS)], axis=0).astype(BF16)
    wq = wq_ref[...] * (IDX_HEADS ** -0.5 * IDX_DIM ** -0.5)
    t_idx = i * Q_BLOCK + lax.broadcasted_iota(jnp.int32, (Q_BLOCK, 1), 0)

    def body(s_len):
        s_idx = lax.broadcasted_iota(jnp.int32, (1, s_len), 1)
        causal = s_idx <= t_idx

        sh_all = _dot(qi_all, ki_t_s[:, 0:s_len])
        score = jnp.zeros((Q_BLOCK, s_len), F32)
        for h in range(IDX_HEADS):
            sh = sh_all[h * Q_BLOCK:(h + 1) * Q_BLOCK, :]
            score = score + wq[:, IDX_DIM + h:IDX_DIM + h + 1] * jnp.maximum(sh, 0.0)
        masked = jnp.where(causal, score, NEG_INF)
        score_s[:, 0:s_len] = masked
        for c0 in range(0, s_len, LANES):
            score_t_s[c0:c0 + LANES, :] = masked[:, c0:c0 + LANES].T

        def count_t(p_row):
            acc = jnp.zeros((COUNT_SLAB, Q_BLOCK), F32)
            for r0 in range(0, s_len, COUNT_SLAB):
                acc = acc + jnp.where(score_t_s[r0:r0 + COUNT_SLAB, :] >= p_row, 1.0, 0.0)
            return jnp.sum(acc, axis=0, keepdims=True)

        def step(it, tu):
            bit = lax.shift_left(jnp.int32(1), 31 - it)
            cand = tu | bit
            p = _key_to_float(cand ^ INT_MIN)
            return jnp.where(count_t(p) >= kf, cand, tu)

        tu = lax.fori_loop(0, 32, step, jnp.zeros((1, Q_BLOCK), jnp.int32))
        key = tu ^ INT_MIN
        short_row = (i * Q_BLOCK + lane + 1).astype(F32) <= kf
        lo_row = jnp.where(short_row, F32_LOWEST, _key_to_float(key))
        hi_row = _key_to_float(key + 1)
        cnt_lo_row = jnp.where(short_row, kf, count_t(lo_row))
        eye = (lax.broadcasted_iota(jnp.int32, (Q_BLOCK, Q_BLOCK), 0)
               == lax.broadcasted_iota(jnp.int32, (Q_BLOCK, Q_BLOCK), 1))

        def to_col(row):
            return jnp.sum(jnp.where(eye, row, 0.0), axis=-1, keepdims=True)

        lo = to_col(lo_row)
        bias_s[:, 0:s_len] = jnp.where(score_s[:, 0:s_len] >= lo, 0.0, NEG_INF)

        unresolved = jnp.max(jnp.where(cnt_lo_row != kf, 1.0, 0.0)) > 0.0

        @pl.when(unresolved)
        def _ties():
            hi = to_col(hi_row)
            cnt_lo = to_col(cnt_lo_row)

            def active(lo_, hi_, c_):
                mid = lo_ + (hi_ - lo_) * 0.5
                act = jnp.where(c_ != kf, jnp.where(mid > lo_, jnp.where(mid < hi_, 1.0, 0.0), 0.0), 0.0)
                return mid, act

            def cond(c):
                return c[4] > 0

            def wbody(c):
                lo_, hi_, c_, it, _ = c
                mid, act = active(lo_, hi_, c_)
                cnt = _count_ge(score_s, s_len, mid)
                up = jnp.where(cnt >= kf, act, 0.0) > 0.0
                dn = jnp.where(cnt >= kf, 0.0, act) > 0.0
                lo2 = jnp.where(up, mid, lo_)
                c2 = jnp.where(up, cnt, c_)
                hi2 = jnp.where(dn, mid, hi_)
                _, act2 = active(lo2, hi2, c2)
                go = jnp.where(jnp.max(act2) > 0.0, 1, 0) * jnp.where(it < 200, 1, 0)
                return lo2, hi2, c2, it + 1, go.astype(jnp.int32)

            _, act0 = active(lo, hi, cnt_lo)
            go0 = jnp.where(jnp.max(act0) > 0.0, 1, 0).astype(jnp.int32)
            lo_f, _, _, _, _ = lax.while_loop(cond, wbody, (lo, hi, cnt_lo, jnp.int32(0), go0))

            sc = score_s[:, 0:s_len]
            gt = sc > lo_f
            need = kf - jnp.sum(jnp.where(gt, 1.0, 0.0), axis=-1, keepdims=True)
            eq = jnp.where(sc == lo_f, 1.0, 0.0)
            rr = lax.broadcasted_iota(jnp.int32, (LANES, LANES), 0)
            cc = lax.broadcasted_iota(jnp.int32, (LANES, LANES), 1)
            upper = jnp.where(rr < cc, 1.0, 0.0).astype(BF16)
            carry = jnp.zeros((Q_BLOCK, 1), F32)
            for c0 in range(0, s_len, LANES):
                eq_c = eq[:, c0:c0 + LANES]
                prefix = _dot(eq_c.astype(BF16), upper) + carry
                keep = jnp.where(gt[:, c0:c0 + LANES], 1.0, jnp.where(prefix < need, eq_c, 0.0))
                bias_s[:, c0:c0 + LANES] = jnp.where(keep > 0.0, 0.0, NEG_INF)
                carry = carry + jnp.sum(eq_c, axis=-1, keepdims=True)

        key_pos = (s_idx - i * Q_BLOCK).astype(F32)
        low = lane < HEAD_DIM
        for pr_i in range(B_HEADS // 2):
            q2 = jnp.concatenate(q_heads[2 * pr_i:2 * pr_i + 2], axis=0).astype(BF16)
            qk = _dot(q2, kb_t_s[:, 0:s_len])
            slopes = [2.0 ** (-8.0 * (2 * pr_i + e + 1) / B_HEADS) * LOG2E for e in range(2)]
            lgs = [qk[e * Q_BLOCK:(e + 1) * Q_BLOCK, :] + slopes[e] * key_pos + bias_s[:, 0:s_len]
                   for e in range(2)]
            ms = [jnp.max(lg, axis=-1, keepdims=True) for lg in lgs]
            prs = [jnp.exp2(lg - m).astype(BF16) for lg, m in zip(lgs, ms)]
            pv = _dot(jnp.concatenate(prs, axis=0), v_s[0:s_len, :])
            outs = [pv[e * Q_BLOCK:(e + 1) * Q_BLOCK, :] for e in range(2)]
            outs = [o / o[:, 0:1] for o in outs]
            o_ref[:, pr_i * LANES:(pr_i + 1) * LANES] = jnp.where(
                low, pltpu.roll(outs[0], HEAD_DIM, axis=1), outs[1])

    per = (SEQ // Q_BLOCK) // len(buckets)
    for bi, s_len in enumerate(buckets):
        pl.when(i // per == bi)(functools.partial(body, s_len))


def _dsa(h0, gq_tile, gk_pad, gi_pad, buckets):
    h3 = h0.reshape(BATCH, SEQ, E_COLS)
    nq = SEQ // Q_BLOCK
    return pl.pallas_call(
        functools.partial(_dsa_kernel, buckets=buckets),
        grid=(BATCH, nq),
        in_specs=[
            pl.BlockSpec((None, Q_BLOCK, B_WIDTH), lambda b, i: (b, i, E_QB // B_WIDTH)),
            pl.BlockSpec((None, SEQ, LANES), lambda b, i: (b, 0, E_KV // LANES)),
            pl.BlockSpec((None, Q_BLOCK, 2 * LANES), lambda b, i: (b, i, E_QI // (2 * LANES))),
            pl.BlockSpec((None, SEQ, LANES), lambda b, i: (b, 0, E_KIW // LANES)),
            pl.BlockSpec((None, Q_BLOCK, LANES), lambda b, i: (b, i, E_KIW // LANES)),
            pl.BlockSpec((1, B_WIDTH), lambda b, i: (0, 0)),
            pl.BlockSpec((1, LANES), lambda b, i: (0, 0)),
            pl.BlockSpec((1, LANES), lambda b, i: (0, 0)),
        ],
        out_specs=pl.BlockSpec((None, Q_BLOCK, B_WIDTH), lambda b, i: (b, i, 0)),
        out_shape=jax.ShapeDtypeStruct((BATCH, SEQ, B_WIDTH), F32),
        scratch_shapes=[
            pltpu.VMEM((IDX_DIM, SEQ), BF16),
            pltpu.VMEM((HEAD_DIM, SEQ), BF16),
            pltpu.VMEM((SEQ, LANES), BF16),
            pltpu.VMEM((Q_BLOCK, SEQ), F32),
            pltpu.VMEM((SEQ, Q_BLOCK), F32),
            pltpu.VMEM((Q_BLOCK, SEQ), F32),
        ],
        compiler_params=_params("arbitrary", "arbitrary"),
        name="dsa",
    )(h3, h3, h3, h3, h3, gq_tile, gk_pad, gi_pad).reshape(BATCH * SEQ, B_WIDTH)


def _outproj_even_kernel(ya, za, yb, zb, x, w, out):
    ga = (ya[...] * _silu(za[...])).astype(BF16)
    gb = (yb[...] * _silu(zb[...])).astype(BF16)
    out[...] = x[...] + _dot(ga, w[0:A_WIDTH, :]) + _dot(gb, w[A_WIDTH:, :])


def _outproj_even(ya, h0, yb, x2d, w_bf16):
    m = x2d.shape[0]
    half = pl.BlockSpec((ROW_TILE, A_WIDTH), lambda i: (i, 0))
    return pl.pallas_call(
        _outproj_even_kernel,
        grid=(m // ROW_TILE,),
        in_specs=[
            half,
            pl.BlockSpec((ROW_TILE, A_WIDTH), lambda i: (i, E_ZA // A_WIDTH)),
            half,
            pl.BlockSpec((ROW_TILE, B_WIDTH), lambda i: (i, E_ZB // B_WIDTH)),
            pl.BlockSpec((ROW_TILE, D_MODEL), lambda i: (i, 0)),
            pl.BlockSpec((A_WIDTH + B_WIDTH, D_MODEL), lambda i: (0, 0)),
        ],
        out_specs=pl.BlockSpec((ROW_TILE, D_MODEL), lambda i: (i, 0)),
        out_shape=jax.ShapeDtypeStruct((m, D_MODEL), F32),
        compiler_params=_params("arbitrary"),
        name="outproj_even",
    )(ya, h0, yb, h0, x2d, w_bf16)


def _forget_kernel(fg_ref, b_ref, crow_ref):
    x = fg_ref[...] + b_ref[...]
    logf = jnp.minimum(x, 0.0) - jnp.log1p(jnp.exp(-jnp.abs(x)))
    rr = lax.broadcasted_iota(jnp.int32, (LANES, LANES), 0)
    cc = lax.broadcasted_iota(jnp.int32, (LANES, LANES), 1)
    tri = jnp.where(cc <= rr, 1.0, 0.0).astype(BF16)
    carry = jnp.zeros((1, LANES), F32)
    for blk in range(SEQ // LANES):
        rows = slice(blk * LANES, (blk + 1) * LANES)
        v = logf[rows, :]
        t0 = v.astype(BF16)
        r1 = v - t0.astype(F32)
        t1 = r1.astype(BF16)
        t2 = (r1 - t1.astype(F32)).astype(BF16)
        c = _dot(tri, t0) + _dot(tri, t1) + _dot(tri, t2) + carry
        crow_ref[:, rows] = c.T[0:C_HEADS, :]
        carry = c[LANES - 1:LANES, :]


def _forget_cumsum(h1, b_pad):
    h3 = h1.reshape(BATCH, SEQ, O_COLS)
    return pl.pallas_call(
        _forget_kernel,
        grid=(BATCH,),
        in_specs=[
            pl.BlockSpec((None, SEQ, LANES), lambda b: (b, 0, O_FG // LANES)),
            pl.BlockSpec((1, LANES), lambda b: (0, 0)),
        ],
        out_specs=pl.BlockSpec((None, C_HEADS, SEQ), lambda b: (b, 0, 0)),
        out_shape=jax.ShapeDtypeStruct((BATCH, C_HEADS, SEQ), F32),
        compiler_params=_params("arbitrary"),
        name="forget_cumsum",
    )(h3, b_pad)


def _fox_kernel(q_ref, k_ref, v_ref, crow_ref, gq_ref, gk_ref, y_ref, qn_s, kp_t_s, vp_s):
    lane = _lane_iota((1, LANES))
    low = lane < HEAD_DIM
    qn_s[...] = (_head_rms_norm(q_ref[...], gq_ref[...]) * (HEAD_DIM ** -0.5 * LOG2E)).astype(BF16)
    kn = _head_rms_norm(k_ref[...], gk_ref[...])
    v = v_ref[...]
    for c0 in range(0, SEQ, LANES):
        kc = kn[c0:c0 + LANES, :]
        kp_t_s[0, :, c0:c0 + LANES] = jnp.where(low, kc, 0.0).T.astype(BF16)
        kp_t_s[1, :, c0:c0 + LANES] = jnp.where(low, 0.0, kc).T.astype(BF16)
    vp_s[0] = _with_ones_lane(jnp.where(low, v, 0.0), lane, 0)
    vp_s[1] = _with_ones_lane(jnp.where(low, 0.0, v), lane, 1)

    crow = crow_ref[...]
    on_or_below = (lax.broadcasted_iota(jnp.int32, (FOX_Q_BLOCK, FOX_Q_BLOCK), 1)
                   <= lax.broadcasted_iota(jnp.int32, (FOX_Q_BLOCK, FOX_Q_BLOCK), 0))
    for iq in range(SEQ // FOX_Q_BLOCK):
        rows = slice(iq * FOX_Q_BLOCK, (iq + 1) * FOX_Q_BLOCK)
        s_len = (iq + 1) * FOX_Q_BLOCK
        d0 = s_len - FOX_Q_BLOCK
        qblk = qn_s[rows, :]
        lgs = []
        for e in range(2):
            decay = (crow[e:e + 1, d0:d0 + 1] - crow[e:e + 1, 0:s_len]) * LOG2E
            lg = _dot(qblk, kp_t_s[e, :, 0:s_len]) + decay
            tail = jnp.where(on_or_below, lg[:, d0:], NEG_INF)
            lgs.append(tail if d0 == 0 else jnp.concatenate([lg[:, :d0], tail], axis=-1))
        ms = [jnp.max(lg, axis=-1, keepdims=True) for lg in lgs]
        prs = [jnp.exp2(lg - m) for lg, m in zip(lgs, ms)]
        y_ref[rows, :] = _pair_output(prs, vp_s, s_len, low)


def _fox(h1, crow, gq_tile, gk_tile):
    h3 = h1.reshape(BATCH, SEQ, O_COLS)
    crow4 = crow.reshape(BATCH, C_HEADS // 2, 2, SEQ)
    blk = lambda off: pl.BlockSpec((None, SEQ, LANES), lambda b, p: (b, 0, off // LANES + p))
    par = pl.BlockSpec((1, LANES), lambda b, p: (0, 0))
    return pl.pallas_call(
        _fox_kernel,
        grid=(BATCH, C_HEADS // 2),
        in_specs=[
            blk(O_Q), blk(O_K), blk(O_V),
            pl.BlockSpec((None, None, 2, SEQ), lambda b, p: (b, p, 0, 0)),
            par, par,
        ],
        out_specs=pl.BlockSpec((None, SEQ, LANES), lambda b, p: (b, 0, p)),
        out_shape=jax.ShapeDtypeStruct((BATCH, SEQ, C_WIDTH), F32),
        scratch_shapes=[
            pltpu.VMEM((SEQ, LANES), BF16),
            pltpu.VMEM((2, LANES, SEQ), BF16),
            pltpu.VMEM((2, SEQ, LANES), BF16),
        ],
        compiler_params=_params("arbitrary", "arbitrary"),
        name="fox",
    )(h3, h3, h3, crow4, gq_tile, gk_tile).reshape(BATCH * SEQ, C_WIDTH)


def _outproj_odd_kernel(y, z, x, w, out):
    g = (y[...] * _silu(z[...])).astype(BF16)
    out[...] = x[...] + _dot(g, w[...])


def _outproj_odd(y, h1, x2d, w_bf16):
    m = x2d.shape[0]
    full = pl.BlockSpec((ROW_TILE, D_MODEL), lambda i: (i, 0))
    return pl.pallas_call(
        _outproj_odd_kernel,
        grid=(m // ROW_TILE,),
        in_specs=[
            full,
            pl.BlockSpec((ROW_TILE, C_WIDTH), lambda i: (i, O_Z // C_WIDTH)),
            full,
            pl.BlockSpec((C_WIDTH, D_MODEL), lambda i: (0, 0)),
        ],
        out_specs=full,
        out_shape=jax.ShapeDtypeStruct((m, D_MODEL), F32),
        compiler_params=_params("arbitrary"),
        name="outproj_odd",
    )(y, h1, x2d, w_bf16)


def _even_weight(w):
    qa, ka, va, za, qb, kb, vb, zb, qi, ki, wi = jnp.split(
        w, np.cumsum([512, 512, 512, 512, 512, 64, 64, 512, 256, 32, 8])[:-1].tolist(), axis=-1)
    pad = jnp.zeros((w.shape[0], E_COLS - (E_KIW + IDX_DIM + IDX_HEADS)), w.dtype)
    return jnp.concatenate([qa, ka, va, za, qb, zb, qi, kb, vb, ki, wi, pad], axis=-1).astype(BF16)


def _pad_lanes(v, width):
    return jnp.pad(v, (0, width - v.shape[0])).reshape(1, width)


def _tile_heads(g, heads):
    return jnp.tile(g, heads).reshape(1, heads * HEAD_DIM)


def kernel(x, even_norm, even_w_in, even_q_norm_a, even_k_norm_a, even_q_norm_b, even_k_norm_b,
           even_k_norm_idx, even_w_out, odd_norm, odd_w_in, odd_b_forget, odd_q_norm, odd_k_norm,
           odd_w_out):
    assert x.shape == (BATCH, SEQ, D_MODEL)
    x2d = x.reshape(BATCH * SEQ, D_MODEL)

    h0 = _inproj(x2d, even_norm[0].reshape(1, D_MODEL), _even_weight(even_w_in[0]), chunk=512)
    slope_tile = jnp.asarray(
        np.repeat([2.0 ** (-8.0 * (i + 1) / A_HEADS) for i in range(A_HEADS)], HEAD_DIM), F32
    ).reshape(1, A_WIDTH)
    gqa = _tile_heads(even_q_norm_a[0], A_HEADS)
    gka = _tile_heads(even_k_norm_a[0], A_HEADS)
    ya = _dilated(h0, gqa, gka, slope_tile)
    yb = _dsa(
        h0,
        _tile_heads(even_q_norm_b[0], B_HEADS),
        _pad_lanes(even_k_norm_b[0], LANES),
        _pad_lanes(even_k_norm_idx[0], LANES),
        buckets=tuple(range(Q_BLOCK, SEQ + 1, Q_BLOCK)),
    )
    x1 = _outproj_even(ya, h0, yb, x2d, even_w_out[0].astype(BF16))

    w1 = jnp.pad(odd_w_in[0], ((0, 0), (0, O_COLS - odd_w_in.shape[-1]))).astype(BF16)
    h1 = _inproj(x1, odd_norm[0].reshape(1, D_MODEL), w1, chunk=512)
    crow = _forget_cumsum(h1, _pad_lanes(odd_b_forget[0], LANES))
    y = _fox(h1, crow, _tile_heads(odd_q_norm[0], 2), _tile_heads(odd_k_norm[0], 2))
    out = _outproj_odd(y, h1, x1, odd_w_out[0].astype(BF16))
    return out.reshape(BATCH, SEQ, D_MODEL)
```

```python
import functools

import numpy as np
import jax
import jax.numpy as jnp
from jax import lax
from jax.experimental import pallas as pl
from jax.experimental.pallas import tpu as pltpu

F32 = jnp.float32
BF16 = jnp.bfloat16

D_MODEL = 1024
BATCH = 16
SEQ = 2048
HEAD_DIM = 64
Q_BLOCK = 128
QUARTER = SEQ // 4
EPS = 1e-6
LANES = 128

A_HEADS = 8
A_GROUPS = ((128, 1), (512, 4), (2048, 16))
A_WIDTH = A_HEADS * HEAD_DIM
B_HEADS = 8
B_WIDTH = B_HEADS * HEAD_DIM
IDX_HEADS = 8
IDX_DIM = 32
IDX_TOPK = 256
C_HEADS = 16
C_WIDTH = C_HEADS * HEAD_DIM

E_QA, E_KA, E_VA, E_ZA = 0, 512, 1024, 1536
E_QB, E_ZB, E_QI, E_KV, E_KIW = 2048, 2560, 3072, 3328, 3456
E_COLS = 3584
O_Q, O_K, O_V, O_Z, O_FG = 0, 1024, 2048, 3072, 4096
O_COLS = 4224

ROW_TILE = 512
VMEM_LIMIT = 48 * 1024 * 1024
NEG_INF = float("-inf")
F32_LOWEST = float(np.finfo(np.float32).min)
INT_MIN = -(2 ** 31)
LOG2E = float(np.log2(np.e))
FIRST_PER_ITER = 8
REST_PER_ITER = {1: 5, 4: 6}
FOX_Q_BLOCK = 256
COUNT_SLAB = 64

_NT = (((1,), (1,)), ((), ()))


def _params(*sem):
    return pltpu.CompilerParams(dimension_semantics=sem, vmem_limit_bytes=VMEM_LIMIT)


def _dot(a, b):
    return jnp.dot(a, b, preferred_element_type=F32)


def _dot_nt(a, b):
    return lax.dot_general(a, b, _NT, preferred_element_type=F32)


def _lane_iota(shape):
    return lax.broadcasted_iota(jnp.int32, shape, len(shape) - 1)


def _segment_mean_sq(x, seg):
    r = lax.broadcasted_iota(jnp.int32, (LANES, LANES), 0) // seg
    c = lax.broadcasted_iota(jnp.int32, (LANES, LANES), 1) // seg
    bd = jnp.where(r == c, 1.0 / seg, 0.0).astype(BF16)
    xx = x * x
    hi = xx.astype(BF16)
    lo = (xx - hi.astype(F32)).astype(BF16)
    return _dot(hi, bd) + _dot(lo, bd)


def _head_rms_norm(x, g):
    w = x.shape[-1]
    tiles = []
    for c in range(0, w, LANES):
        xt = x[:, c:c + LANES]
        ms = _segment_mean_sq(xt, HEAD_DIM)
        tiles.append(xt * lax.rsqrt(ms + EPS) * g[:, c:c + LANES])
    return tiles[0] if len(tiles) == 1 else jnp.concatenate(tiles, axis=-1)


def _with_ones_lane(v_placed, lane, head):
    return jnp.where(lane == _ones_lane(head), 1.0, v_placed).astype(BF16)


def _ones_lane(head):
    return HEAD_DIM if head == 0 else 0


def _pair_output(prs, vp_s, s_len, low):
    pv = [_dot(prs[e].astype(BF16), vp_s[e, 0:s_len, :]) for e in range(2)]
    ls = [pv[e][:, _ones_lane(e):_ones_lane(e) + 1] for e in range(2)]
    return jnp.where(low, pv[0] / ls[0], pv[1] / ls[1])


def _silu(z):
    return z / (1.0 + jnp.exp(-z))


def _inproj_kernel(x_ref, g_ref, w_ref, o_ref, *, chunk):
    x = x_ref[...]
    ms = jnp.mean(x * x, axis=-1, keepdims=True)
    xn = (x * lax.rsqrt(ms + EPS) * g_ref[...]).astype(BF16)
    n = o_ref.shape[-1]
    for c in range(0, n, chunk):
        hi = min(c + chunk, n)
        o_ref[:, c:hi] = _dot(xn, w_ref[:, c:hi])


def _inproj(x2d, g, w_bf16, chunk):
    m, d = x2d.shape
    n = w_bf16.shape[1]
    return pl.pallas_call(
        functools.partial(_inproj_kernel, chunk=chunk),
        grid=(m // ROW_TILE,),
        in_specs=[
            pl.BlockSpec((ROW_TILE, d), lambda i: (i, 0)),
            pl.BlockSpec((1, d), lambda i: (0, 0)),
            pl.BlockSpec((d, n), lambda i: (0, 0), pipeline_mode=pl.Buffered(1)),
        ],
        out_specs=pl.BlockSpec((ROW_TILE, n), lambda i: (i, 0)),
        out_shape=jax.ShapeDtypeStruct((m, n), F32),
        compiler_params=_params("arbitrary"),
        name="inproj",
    )(x2d, g, w_bf16)


def _rows(start, size, stride):
    return pl.ds(start, size) if stride == 1 else pl.ds(start, size, stride=stride)


def _dilated_kernel(q_ref, k_ref, v_ref, gq_ref, gk_ref, sl_ref, y_ref,
                    qn_s, kn_s, q4_s, k4_s, v4_s, pv_s, m_s, l_s):
    qn_s[...] = _head_rms_norm(q_ref[...], gq_ref[...]) * (HEAD_DIM ** -0.5 * LOG2E)
    kn_s[...] = _head_rms_norm(k_ref[...], gk_ref[...])
    for c in range(4):
        dst = slice(c * QUARTER, (c + 1) * QUARTER)
        src = _rows(c, QUARTER, 4)
        q4_s[dst, :] = qn_s[src, :]
        k4_s[dst, :] = kn_s[src, :]
        v4_s[dst, :] = v_ref[src, :]

    lane = _lane_iota((1, LANES))
    low = lane < HEAD_DIM
    qi = lax.broadcasted_iota(jnp.int32, (Q_BLOCK, 2 * Q_BLOCK), 0)
    kj = lax.broadcasted_iota(jnp.int32, (Q_BLOCK, 2 * Q_BLOCK), 1)
    dsub = Q_BLOCK + qi - kj
    slopes = sl_ref[...] * LOG2E
    slope_pair = (slopes[:, 0:1], slopes[:, HEAD_DIM:HEAD_DIM + 1])

    for g, (window, dil) in enumerate(A_GROUPS):
        span = window // dil
        nb = SEQ // dil // Q_BLOCK
        base = jnp.where(dsub >= 0, jnp.where(dsub <= span, (-dil * dsub).astype(F32), NEG_INF), NEG_INF)
        biases = (base * slope_pair[0], base * slope_pair[1])
        q_src, k_src, v_src = (qn_s, kn_s, v_ref) if g == 0 else (q4_s, k4_s, v4_s)
        stride = 4 if dil == 16 else 1

        def load(q0, first, stride=stride, q_src=q_src, k_src=k_src, v_src=v_src):
            qr = _rows(q0, Q_BLOCK, stride)
            kr = qr if first else _rows(q0 - stride * Q_BLOCK, 2 * Q_BLOCK, stride)
            return q_src[qr, :], k_src[kr, :].astype(BF16), v_src[kr, :].astype(BF16)

        def compute(loaded, first, biases=biases):
            n = len(loaded)
            lgs = [
                _dot_nt(jnp.where(low if e == 0 else ~low, qb, 0.0).astype(BF16), kk)
                + (biases[e][:, Q_BLOCK:] if first else biases[e])
                for qb, kk, _ in loaded for e in range(2)
            ]
            ms = [jnp.max(lg, axis=-1, keepdims=True) for lg in lgs]
            prs = [jnp.exp2(lg - m) for lg, m in zip(lgs, ms)]
            ls = [jnp.sum(pr, axis=-1, keepdims=True) for pr in prs]
            pvs = [_dot(pr.astype(BF16), loaded[i // 2][2]) for i, pr in enumerate(prs)]
            return [
                (jnp.where(low, pvs[2 * b], pvs[2 * b + 1]), jnp.where(low, ms[2 * b], ms[2 * b + 1]),
                 jnp.where(low, ls[2 * b], ls[2 * b + 1]))
                for b in range(n)
            ]

        def store(q0, res, g=g, stride=stride):
            qr = _rows(q0, Q_BLOCK, stride)
            pv_s[g, qr, :] = res[0]
            m_s[g, qr, :] = res[1]
            l_s[g, qr, :] = res[2]

        def run_blocks(start_of, count, per_iter, first, load=load, compute=compute, store=store):
            def body(t, carry):
                starts = [start_of(t * per_iter + u) for u in range(per_iter)]
                loaded = [load(q0, first) for q0 in starts]
                results = compute(loaded, first)
                for q0, res in zip(starts, results):
                    store(q0, res)
                return carry
            trips = count // per_iter
            if trips == 1:
                body(0, 0)
            else:
                lax.fori_loop(0, trips, body, 0)

        per = nb - 1
        if dil == 1:
            first_start = lambda idx: 0
            rest_start = lambda idx: Q_BLOCK * (1 + idx)
        elif dil == 4:
            first_start = lambda idx: idx * QUARTER
            rest_start = lambda idx, per=per: lax.div(idx, per) * QUARTER + Q_BLOCK * (1 + lax.rem(idx, per))
        else:
            first_start = lambda idx: lax.rem(idx, 4) * QUARTER + lax.div(idx, 4)
            rest_start = None
        run_blocks(first_start, dil, min(dil, FIRST_PER_ITER), True)
        if per:
            run_blocks(rest_start, dil * per, REST_PER_ITER[dil], False)

    chunk = 2 * Q_BLOCK
    for c in range(4):
        for o in range(0, QUARTER, chunk):
            nat = _rows(c + 4 * o, chunk, 4)
            grp = slice(c * QUARTER + o, c * QUARTER + o + chunk)
            m0, m1, m2 = m_s[0, nat, :], m_s[1, grp, :], m_s[2, grp, :]
            mm = jnp.maximum(jnp.maximum(m0, m1), m2)
            w0, w1, w2 = jnp.exp2(m0 - mm), jnp.exp2(m1 - mm), jnp.exp2(m2 - mm)
            num = w0 * pv_s[0, nat, :] + w1 * pv_s[1, grp, :] + w2 * pv_s[2, grp, :]
            den = w0 * l_s[0, nat, :] + w1 * l_s[1, grp, :] + w2 * l_s[2, grp, :]
            y_ref[nat, :] = num / den


def _dilated(h0, gq_tile, gk_tile, slope_tile):
    h3 = h0.reshape(BATCH, SEQ, E_COLS)
    blk = lambda off: pl.BlockSpec((None, SEQ, LANES), lambda b, p: (b, 0, off // LANES + p))
    par = pl.BlockSpec((1, LANES), lambda b, p: (0, p))
    seq_tile = pltpu.VMEM((SEQ, LANES), F32)
    grp_tile = pltpu.VMEM((len(A_GROUPS), SEQ, LANES), F32)
    return pl.pallas_call(
        _dilated_kernel,
        grid=(BATCH, A_WIDTH // LANES),
        in_specs=[blk(E_QA), blk(E_KA), blk(E_VA), par, par, par],
        out_specs=pl.BlockSpec((None, SEQ, LANES), lambda b, p: (b, 0, p)),
        out_shape=jax.ShapeDtypeStruct((BATCH, SEQ, A_WIDTH), F32),
        scratch_shapes=[seq_tile] * 5 + [grp_tile] * 3,
        compiler_params=_params("arbitrary", "arbitrary"),
        name="dilated",
    )(h3, h3, h3, gq_tile, gk_tile, slope_tile).reshape(BATCH * SEQ, A_WIDTH)


def _count_ge(score_ref, s_len, p):
    x = score_ref[:, 0:s_len]
    return jnp.sum(jnp.where(x >= p, 1.0, 0.0), axis=-1, keepdims=True)


def _key_to_float(key):
    bits = key ^ ((key >> 31) & 0x7FFFFFFF)
    return lax.bitcast_convert_type(bits, F32)


def _dsa_kernel(qb_ref, kv_ref, qi_ref, kiw_ref, wq_ref, gq_ref, gk_ref, gi_ref, o_ref,
                ki_t_s, kb_t_s, v_s, score_s, score_t_s, bias_s, *, buckets):
    i = pl.program_id(1)
    lane = _lane_iota((1, LANES))
    kf = float(IDX_TOPK)

    @pl.when(i == 0)
    def _prep():
        kiw = kiw_ref[...]
        ki = jnp.where(lane < IDX_DIM, kiw, 0.0)
        ms = jnp.sum(ki * ki, axis=-1, keepdims=True) * (1.0 / IDX_DIM)
        kin = ki * lax.rsqrt(ms + EPS) * gi_ref[...]
        kv = kv_ref[...]
        kb = jnp.where(lane < HEAD_DIM, kv, 0.0)
        ms = jnp.sum(kb * kb, axis=-1, keepdims=True) * (1.0 / HEAD_DIM)
        kbn = kb * lax.rsqrt(ms + EPS) * gk_ref[...]
        for c0 in range(0, SEQ, LANES):
            ki_t_s[:, c0:c0 + LANES] = kin[c0:c0 + LANES, :].T[0:IDX_DIM, :].astype(BF16)
            kb_t_s[:, c0:c0 + LANES] = kbn[c0:c0 + LANES, :].T[0:HEAD_DIM, :].astype(BF16)
        v_s[...] = _with_ones_lane(jnp.where(lane >= HEAD_DIM, kv, 0.0), lane, 1)

    qn = _head_rms_norm(qb_ref[...], gq_ref[...]) * (HEAD_DIM ** -0.5 * LOG2E)
    q_heads = [qn[:, h * HEAD_DIM:(h + 1) * HEAD_DIM] for h in range(B_HEADS)]
    qraw = qi_ref[...]
    qi_all = jnp.concatenate(
        [qraw[:, h * IDX_DIM:(h + 1) * IDX_DIM] for h in range(IDX_HEADS)], axis=0).astype(BF16)
    wq = wq_ref[...] * (IDX_HEADS ** -0.5 * IDX_DIM ** -0.5)
    t_idx = i * Q_BLOCK + lax.broadcasted_iota(jnp.int32, (Q_BLOCK, 1), 0)

    def body(s_len):
        s_idx = lax.broadcasted_iota(jnp.int32, (1, s_len), 1)
        causal = s_idx <= t_idx

        sh_all = _dot(qi_all, ki_t_s[:, 0:s_len])
        score = jnp.zeros((Q_BLOCK, s_len), F32)
        for h in range(IDX_HEADS):
            sh = sh_all[h * Q_BLOCK:(h + 1) * Q_BLOCK, :]
            score = score + wq[:, IDX_DIM + h:IDX_DIM + h + 1] * jnp.maximum(sh, 0.0)
        masked = jnp.where(causal, score, NEG_INF)
        score_s[:, 0:s_len] = masked
        for c0 in range(0, s_len, LANES):
            score_t_s[c0:c0 + LANES, :] = masked[:, c0:c0 + LANES].T

        def count_t(p_row):
            acc = jnp.zeros((COUNT_SLAB, Q_BLOCK), F32)
            for r0 in range(0, s_len, COUNT_SLAB):
                acc = acc + jnp.where(score_t_s[r0:r0 + COUNT_SLAB, :] >= p_row, 1.0, 0.0)
            return jnp.sum(acc, axis=0, keepdims=True)

        def step(it, tu):
            bit = lax.shift_left(jnp.int32(1), 31 - it)
            cand = tu | bit
            p = _key_to_float(cand ^ INT_MIN)
            return jnp.where(count_t(p) >= kf, cand, tu)

        tu = lax.fori_loop(0, 32, step, jnp.zeros((1, Q_BLOCK), jnp.int32))
        key = tu ^ INT_MIN
        short_row = (i * Q_BLOCK + lane + 1).astype(F32) <= kf
        lo_row = jnp.where(short_row, F32_LOWEST, _key_to_float(key))
        hi_row = _key_to_float(key + 1)
        cnt_lo_row = jnp.where(short_row, kf, count_t(lo_row))
        eye = (lax.broadcasted_iota(jnp.int32, (Q_BLOCK, Q_BLOCK), 0)
               == lax.broadcasted_iota(jnp.int32, (Q_BLOCK, Q_BLOCK), 1))

        def to_col(row):
            return jnp.sum(jnp.where(eye, row, 0.0), axis=-1, keepdims=True)

        lo = to_col(lo_row)
        bias_s[:, 0:s_len] = jnp.where(score_s[:, 0:s_len] >= lo, 0.0, NEG_INF)

        unresolved = jnp.max(jnp.where(cnt_lo_row != kf, 1.0, 0.0)) > 0.0

        @pl.when(unresolved)
        def _ties():
            hi = to_col(hi_row)
            cnt_lo = to_col(cnt_lo_row)

            def active(lo_, hi_, c_):
                mid = lo_ + (hi_ - lo_) * 0.5
                act = jnp.where(c_ != kf, jnp.where(mid > lo_, jnp.where(mid < hi_, 1.0, 0.0), 0.0), 0.0)
                return mid, act

            def cond(c):
                return c[4] > 0

            def wbody(c):
                lo_, hi_, c_, it, _ = c
                mid, act = active(lo_, hi_, c_)
                cnt = _count_ge(score_s, s_len, mid)
                up = jnp.where(cnt >= kf, act, 0.0) > 0.0
                dn = jnp.where(cnt >= kf, 0.0, act) > 0.0
                lo2 = jnp.where(up, mid, lo_)
                c2 = jnp.where(up, cnt, c_)
                hi2 = jnp.where(dn, mid, hi_)
                _, act2 = active(lo2, hi2, c2)
                go = jnp.where(jnp.max(act2) > 0.0, 1, 0) * jnp.where(it < 200, 1, 0)
                return lo2, hi2, c2, it + 1, go.astype(jnp.int32)

            _, act0 = active(lo, hi, cnt_lo)
            go0 = jnp.where(jnp.max(act0) > 0.0, 1, 0).astype(jnp.int32)
            lo_f, _, _, _, _ = lax.while_loop(cond, wbody, (lo, hi, cnt_lo, jnp.int32(0), go0))

            sc = score_s[:, 0:s_len]
            gt = sc > lo_f
            need = kf - jnp.sum(jnp.where(gt, 1.0, 0.0), axis=-1, keepdims=True)
            eq = jnp.where(sc == lo_f, 1.0, 0.0)
            rr = lax.broadcasted_iota(jnp.int32, (LANES, LANES), 0)
            cc = lax.broadcasted_iota(jnp.int32, (LANES, LANES), 1)
            upper = jnp.where(rr < cc, 1.0, 0.0).astype(BF16)
            carry = jnp.zeros((Q_BLOCK, 1), F32)
            for c0 in range(0, s_len, LANES):
                eq_c = eq[:, c0:c0 + LANES]
                prefix = _dot(eq_c.astype(BF16), upper) + carry
                keep = jnp.where(gt[:, c0:c0 + LANES], 1.0, jnp.where(prefix < need, eq_c, 0.0))
                bias_s[:, c0:c0 + LANES] = jnp.where(keep > 0.0, 0.0, NEG_INF)
                carry = carry + jnp.sum(eq_c, axis=-1, keepdims=True)

        key_pos = (s_idx - i * Q_BLOCK).astype(F32)
        low = lane < HEAD_DIM
        for pr_i in range(B_HEADS // 2):
            q2 = jnp.concatenate(q_heads[2 * pr_i:2 * pr_i + 2], axis=0).astype(BF16)
            qk = _dot(q2, kb_t_s[:, 0:s_len])
            slopes = [2.0 ** (-8.0 * (2 * pr_i + e + 1) / B_HEADS) * LOG2E for e in range(2)]
            lgs = [qk[e * Q_BLOCK:(e + 1) * Q_BLOCK, :] + slopes[e] * key_pos + bias_s[:, 0:s_len]
                   for e in range(2)]
            ms = [jnp.max(lg, axis=-1, keepdims=True) for lg in lgs]
            prs = [jnp.exp2(lg - m).astype(BF16) for lg, m in zip(lgs, ms)]
            pv = _dot(jnp.concatenate(prs, axis=0), v_s[0:s_len, :])
            outs = [pv[e * Q_BLOCK:(e + 1) * Q_BLOCK, :] for e in range(2)]
            outs = [o / o[:, 0:1] for o in outs]
            o_ref[:, pr_i * LANES:(pr_i + 1) * LANES] = jnp.where(
                low, pltpu.roll(outs[0], HEAD_DIM, axis=1), outs[1])

    per = (SEQ // Q_BLOCK) // len(buckets)
    for bi, s_len in enumerate(buckets):
        pl.when(i // per == bi)(functools.partial(body, s_len))


def _dsa(h0, gq_tile, gk_pad, gi_pad, buckets):
    h3 = h0.reshape(BATCH, SEQ, E_COLS)
    nq = SEQ // Q_BLOCK
    return pl.pallas_call(
        functools.partial(_dsa_kernel, buckets=buckets),
        grid=(BATCH, nq),
        in_specs=[
            pl.BlockSpec((None, Q_BLOCK, B_WIDTH), lambda b, i: (b, i, E_QB // B_WIDTH)),
            pl.BlockSpec((None, SEQ, LANES), lambda b, i: (b, 0, E_KV // LANES)),
            pl.BlockSpec((None, Q_BLOCK, 2 * LANES), lambda b, i: (b, i, E_QI // (2 * LANES))),
            pl.BlockSpec((None, SEQ, LANES), lambda b, i: (b, 0, E_KIW // LANES)),
            pl.BlockSpec((None, Q_BLOCK, LANES), lambda b, i: (b, i, E_KIW // LANES)),
            pl.BlockSpec((1, B_WIDTH), lambda b, i: (0, 0)),
            pl.BlockSpec((1, LANES), lambda b, i: (0, 0)),
            pl.BlockSpec((1, LANES), lambda b, i: (0, 0)),
        ],
        out_specs=pl.BlockSpec((None, Q_BLOCK, B_WIDTH), lambda b, i: (b, i, 0)),
        out_shape=jax.ShapeDtypeStruct((BATCH, SEQ, B_WIDTH), F32),
        scratch_shapes=[
            pltpu.VMEM((IDX_DIM, SEQ), BF16),
            pltpu.VMEM((HEAD_DIM, SEQ), BF16),
            pltpu.VMEM((SEQ, LANES), BF16),
            pltpu.VMEM((Q_BLOCK, SEQ), F32),
            pltpu.VMEM((SEQ, Q_BLOCK), F32),
            pltpu.VMEM((Q_BLOCK, SEQ), F32),
        ],
        compiler_params=_params("arbitrary", "arbitrary"),
        name="dsa",
    )(h3, h3, h3, h3, h3, gq_tile, gk_pad, gi_pad).reshape(BATCH * SEQ, B_WIDTH)


def _outproj_even_kernel(ya, za, yb, zb, x, w, out):
    ga = (ya[...] * _silu(za[...])).astype(BF16)
    gb = (yb[...] * _silu(zb[...])).astype(BF16)
    out[...] = x[...] + _dot(ga, w[0:A_WIDTH, :]) + _dot(gb, w[A_WIDTH:, :])


def _outproj_even(ya, h0, yb, x2d, w_bf16):
    m = x2d.shape[0]
    half = pl.BlockSpec((ROW_TILE, A_WIDTH), lambda i: (i, 0))
    return pl.pallas_call(
        _outproj_even_kernel,
        grid=(m // ROW_TILE,),
        in_specs=[
            half,
            pl.BlockSpec((ROW_TILE, A_WIDTH), lambda i: (i, E_ZA // A_WIDTH)),
            half,
            pl.BlockSpec((ROW_TILE, B_WIDTH), lambda i: (i, E_ZB // B_WIDTH)),
            pl.BlockSpec((ROW_TILE, D_MODEL), lambda i: (i, 0)),
            pl.BlockSpec((A_WIDTH + B_WIDTH, D_MODEL), lambda i: (0, 0)),
        ],
        out_specs=pl.BlockSpec((ROW_TILE, D_MODEL), lambda i: (i, 0)),
        out_shape=jax.ShapeDtypeStruct((m, D_MODEL), F32),
        compiler_params=_params("arbitrary"),
        name="outproj_even",
    )(ya, h0, yb, h0, x2d, w_bf16)


def _forget_kernel(fg_ref, b_ref, crow_ref):
    x = fg_ref[...] + b_ref[...]
    logf = jnp.minimum(x, 0.0) - jnp.log1p(jnp.exp(-jnp.abs(x)))
    rr = lax.broadcasted_iota(jnp.int32, (LANES, LANES), 0)
    cc = lax.broadcasted_iota(jnp.int32, (LANES, LANES), 1)
    tri = jnp.where(cc <= rr, 1.0, 0.0).astype(BF16)
    carry = jnp.zeros((1, LANES), F32)
    for blk in range(SEQ // LANES):
        rows = slice(blk * LANES, (blk + 1) * LANES)
        v = logf[rows, :]
        t0 = v.astype(BF16)
        r1 = v - t0.astype(F32)
        t1 = r1.astype(BF16)
        t2 = (r1 - t1.astype(F32)).astype(BF16)
        c = _dot(tri, t0) + _dot(tri, t1) + _dot(tri, t2) + carry
        crow_ref[:, rows] = c.T[0:C_HEADS, :]
        carry = c[LANES - 1:LANES, :]


def _forget_cumsum(h1, b_pad):
    h3 = h1.reshape(BATCH, SEQ, O_COLS)
    return pl.pallas_call(
        _forget_kernel,
        grid=(BATCH,),
        in_specs=[
            pl.BlockSpec((None, SEQ, LANES), lambda b: (b, 0, O_FG // LANES)),
            pl.BlockSpec((1, LANES), lambda b: (0, 0)),
        ],
        out_specs=pl.BlockSpec((None, C_HEADS, SEQ), lambda b: (b, 0, 0)),
        out_shape=jax.ShapeDtypeStruct((BATCH, C_HEADS, SEQ), F32),
        compiler_params=_params("arbitrary"),
        name="forget_cumsum",
    )(h3, b_pad)


def _fox_kernel(q_ref, k_ref, v_ref, crow_ref, gq_ref, gk_ref, y_ref, qn_s, kp_t_s, vp_s):
    lane = _lane_iota((1, LANES))
    low = lane < HEAD_DIM
    qn_s[...] = (_head_rms_norm(q_ref[...], gq_ref[...]) * (HEAD_DIM ** -0.5 * LOG2E)).astype(BF16)
    kn = _head_rms_norm(k_ref[...], gk_ref[...])
    v = v_ref[...]
    for c0 in range(0, SEQ, LANES):
        kc = kn[c0:c0 + LANES, :]
        kp_t_s[0, :, c0:c0 + LANES] = jnp.where(low, kc, 0.0).T.astype(BF16)
        kp_t_s[1, :, c0:c0 + LANES] = jnp.where(low, 0.0, kc).T.astype(BF16)
    vp_s[0] = _with_ones_lane(jnp.where(low, v, 0.0), lane, 0)
    vp_s[1] = _with_ones_lane(jnp.where(low, 0.0, v), lane, 1)

    crow = crow_ref[...]
    on_or_below = (lax.broadcasted_iota(jnp.int32, (FOX_Q_BLOCK, FOX_Q_BLOCK), 1)
                   <= lax.broadcasted_iota(jnp.int32, (FOX_Q_BLOCK, FOX_Q_BLOCK), 0))
    for iq in range(SEQ // FOX_Q_BLOCK):
        rows = slice(iq * FOX_Q_BLOCK, (iq + 1) * FOX_Q_BLOCK)
        s_len = (iq + 1) * FOX_Q_BLOCK
        d0 = s_len - FOX_Q_BLOCK
        qblk = qn_s[rows, :]
        lgs = []
        for e in range(2):
            decay = (crow[e:e + 1, d0:d0 + 1] - crow[e:e + 1, 0:s_len]) * LOG2E
            lg = _dot(qblk, kp_t_s[e, :, 0:s_len]) + decay
            tail = jnp.where(on_or_below, lg[:, d0:], NEG_INF)
            lgs.append(tail if d0 == 0 else jnp.concatenate([lg[:, :d0], tail], axis=-1))
        ms = [jnp.max(lg, axis=-1, keepdims=True) for lg in lgs]
        prs = [jnp.exp2(lg - m) for lg, m in zip(lgs, ms)]
        y_ref[rows, :] = _pair_output(prs, vp_s, s_len, low)


def _fox(h1, crow, gq_tile, gk_tile):
    h3 = h1.reshape(BATCH, SEQ, O_COLS)
    crow4 = crow.reshape(BATCH, C_HEADS // 2, 2, SEQ)
    blk = lambda off: pl.BlockSpec((None, SEQ, LANES), lambda b, p: (b, 0, off // LANES + p))
    par = pl.BlockSpec((1, LANES), lambda b, p: (0, 0))
    return pl.pallas_call(
        _fox_kernel,
        grid=(BATCH, C_HEADS // 2),
        in_specs=[
            blk(O_Q), blk(O_K), blk(O_V),
            pl.BlockSpec((None, None, 2, SEQ), lambda b, p: (b, p, 0, 0)),
            par, par,
        ],
        out_specs=pl.BlockSpec((None, SEQ, LANES), lambda b, p: (b, 0, p)),
        out_shape=jax.ShapeDtypeStruct((BATCH, SEQ, C_WIDTH), F32),
        scratch_shapes=[
            pltpu.VMEM((SEQ, LANES), BF16),
            pltpu.VMEM((2, LANES, SEQ), BF16),
            pltpu.VMEM((2, SEQ, LANES), BF16),
        ],
        compiler_params=_params("arbitrary", "arbitrary"),
        name="fox",
    )(h3, h3, h3, crow4, gq_tile, gk_tile).reshape(BATCH * SEQ, C_WIDTH)


def _outproj_odd_kernel(y, z, x, w, out):
    g = (y[...] * _silu(z[...])).astype(BF16)
    out[...] = x[...] + _dot(g, w[...])


def _outproj_odd(y, h1, x2d, w_bf16):
    m = x2d.shape[0]
    full = pl.BlockSpec((ROW_TILE, D_MODEL), lambda i: (i, 0))
    return pl.pallas_call(
        _outproj_odd_kernel,
        grid=(m // ROW_TILE,),
        in_specs=[
            full,
            pl.BlockSpec((ROW_TILE, C_WIDTH), lambda i: (i, O_Z // C_WIDTH)),
            full,
            pl.BlockSpec((C_WIDTH, D_MODEL), lambda i: (0, 0)),
        ],
        out_specs=full,
        out_shape=jax.ShapeDtypeStruct((m, D_MODEL), F32),
        compiler_params=_params("arbitrary"),
        name="outproj_odd",
    )(y, h1, x2d, w_bf16)


def _even_weight(w):
    qa, ka, va, za, qb, kb, vb, zb, qi, ki, wi = jnp.split(
        w, np.cumsum([512, 512, 512, 512, 512, 64, 64, 512, 256, 32, 8])[:-1].tolist(), axis=-1)
    pad = jnp.zeros((w.shape[0], E_COLS - (E_KIW + IDX_DIM + IDX_HEADS)), w.dtype)
    return jnp.concatenate([qa, ka, va, za, qb, zb, qi, kb, vb, ki, wi, pad], axis=-1).astype(BF16)


def _pad_lanes(v, width):
    return jnp.pad(v, (0, width - v.shape[0])).reshape(1, width)


def _tile_heads(g, heads):
    return jnp.tile(g, heads).reshape(1, heads * HEAD_DIM)


def kernel(x, even_norm, even_w_in, even_q_norm_a, even_k_norm_a, even_q_norm_b, even_k_norm_b,
           even_k_norm_idx, even_w_out, odd_norm, odd_w_in, odd_b_forget, odd_q_norm, odd_k_norm,
           odd_w_out):
    assert x.shape == (BATCH, SEQ, D_MODEL)
    x2d = x.reshape(BATCH * SEQ, D_MODEL)

    h0 = _inproj(x2d, even_norm[0].reshape(1, D_MODEL), _even_weight(even_w_in[0]), chunk=512)
    slope_tile = jnp.asarray(
        np.repeat([2.0 ** (-8.0 * (i + 1) / A_HEADS) for i in range(A_HEADS)], HEAD_DIM), F32
    ).reshape(1, A_WIDTH)
    gqa = _tile_heads(even_q_norm_a[0], A_HEADS)
    gka = _tile_heads(even_k_norm_a[0], A_HEADS)
    ya = _dilated(h0, gqa, gka, slope_tile)
    yb = _dsa(
        h0,
        _tile_heads(even_q_norm_b[0], B_HEADS),
        _pad_lanes(even_k_norm_b[0], LANES),
        _pad_lanes(even_k_norm_idx[0], LANES),
        buckets=tuple(range(256, SEQ + 1, 256)),
    )
    x1 = _outproj_even(ya, h0, yb, x2d, even_w_out[0].astype(BF16))

    w1 = jnp.pad(odd_w_in[0], ((0, 0), (0, O_COLS - odd_w_in.shape[-1]))).astype(BF16)
    h1 = _inproj(x1, odd_norm[0].reshape(1, D_MODEL), w1, chunk=512)
    crow = _forget_cumsum(h1, _pad_lanes(odd_b_forget[0], LANES))
    y = _fox(h1, crow, _tile_heads(odd_q_norm[0], 2), _tile_heads(odd_k_norm[0], 2))
    out = _outproj_odd(y, h1, x1, odd_w_out[0].astype(BF16))
    return out.reshape(BATCH, SEQ, D_MODEL)
```

```python
import functools

import numpy as np
import jax
import jax.numpy as jnp
from jax import lax
from jax.experimental import pallas as pl
from jax.experimental.pallas import tpu as pltpu

F32 = jnp.float32
BF16 = jnp.bfloat16

D_MODEL = 1024
BATCH = 16
SEQ = 2048
HEAD_DIM = 64
Q_BLOCK = 128
QUARTER = SEQ // 4
EPS = 1e-6
LANES = 128

A_HEADS = 8
A_GROUPS = ((128, 1), (512, 4), (2048, 16))
A_WIDTH = A_HEADS * HEAD_DIM
B_HEADS = 8
B_WIDTH = B_HEADS * HEAD_DIM
IDX_HEADS = 8
IDX_DIM = 32
IDX_TOPK = 256
C_HEADS = 16
C_WIDTH = C_HEADS * HEAD_DIM

E_QA, E_KA, E_VA, E_ZA = 0, 512, 1024, 1536
E_QB, E_ZB, E_QI, E_KV, E_KIW = 2048, 2560, 3072, 3328, 3456
E_COLS = 3584
O_Q, O_K, O_V, O_Z, O_FG = 0, 1024, 2048, 3072, 4096
O_COLS = 4224

ROW_TILE = 512
FUSED_ROW_TILE = 256
VMEM_LIMIT = 48 * 1024 * 1024
NEG_INF = float("-inf")
F32_LOWEST = float(np.finfo(np.float32).min)
INT_MIN = -(2 ** 31)
LOG2E = float(np.log2(np.e))
FIRST_PER_ITER = 8
REST_PER_ITER = {1: 5, 4: 6}
FOX_Q_BLOCK = 256
COUNT_SLAB = 64

_NT = (((1,), (1,)), ((), ()))


def _params(*sem):
    return pltpu.CompilerParams(dimension_semantics=sem, vmem_limit_bytes=VMEM_LIMIT)


def _dot(a, b):
    return jnp.dot(a, b, preferred_element_type=F32)


def _dot_nt(a, b):
    return lax.dot_general(a, b, _NT, preferred_element_type=F32)


def _lane_iota(shape):
    return lax.broadcasted_iota(jnp.int32, shape, len(shape) - 1)


def _segment_mean_sq(x, seg):
    r = lax.broadcasted_iota(jnp.int32, (LANES, LANES), 0) // seg
    c = lax.broadcasted_iota(jnp.int32, (LANES, LANES), 1) // seg
    bd = jnp.where(r == c, 1.0 / seg, 0.0).astype(BF16)
    xx = x * x
    hi = xx.astype(BF16)
    lo = (xx - hi.astype(F32)).astype(BF16)
    return _dot(hi, bd) + _dot(lo, bd)


def _head_rms_norm(x, g):
    w = x.shape[-1]
    tiles = []
    for c in range(0, w, LANES):
        xt = x[:, c:c + LANES]
        ms = _segment_mean_sq(xt, HEAD_DIM)
        tiles.append(xt * lax.rsqrt(ms + EPS) * g[:, c:c + LANES])
    return tiles[0] if len(tiles) == 1 else jnp.concatenate(tiles, axis=-1)


def _with_ones_lane(v_placed, lane, head):
    return jnp.where(lane == _ones_lane(head), 1.0, v_placed).astype(BF16)


def _ones_lane(head):
    return HEAD_DIM if head == 0 else 0


def _pair_output(prs, vp_s, s_len, low):
    pv = [_dot(prs[e].astype(BF16), vp_s[e, 0:s_len, :]) for e in range(2)]
    ls = [pv[e][:, _ones_lane(e):_ones_lane(e) + 1] for e in range(2)]
    return jnp.where(low, pv[0] / ls[0], pv[1] / ls[1])


def _silu(z):
    return z / (1.0 + jnp.exp(-z))


def _inproj_kernel(x_ref, g_ref, w_ref, o_ref, *, chunk):
    x = x_ref[...]
    ms = jnp.mean(x * x, axis=-1, keepdims=True)
    xn = (x * lax.rsqrt(ms + EPS) * g_ref[...]).astype(BF16)
    n = o_ref.shape[-1]
    for c in range(0, n, chunk):
        hi = min(c + chunk, n)
        o_ref[:, c:hi] = _dot(xn, w_ref[:, c:hi])


def _inproj(x2d, g, w_bf16, chunk):
    m, d = x2d.shape
    n = w_bf16.shape[1]
    return pl.pallas_call(
        functools.partial(_inproj_kernel, chunk=chunk),
        grid=(m // ROW_TILE,),
        in_specs=[
            pl.BlockSpec((ROW_TILE, d), lambda i: (i, 0)),
            pl.BlockSpec((1, d), lambda i: (0, 0)),
            pl.BlockSpec((d, n), lambda i: (0, 0), pipeline_mode=pl.Buffered(1)),
        ],
        out_specs=pl.BlockSpec((ROW_TILE, n), lambda i: (i, 0)),
        out_shape=jax.ShapeDtypeStruct((m, n), F32),
        compiler_params=_params("arbitrary"),
        name="inproj",
    )(x2d, g, w_bf16)


def _rows(start, size, stride):
    return pl.ds(start, size) if stride == 1 else pl.ds(start, size, stride=stride)


def _dilated_kernel(q_ref, k_ref, v_ref, gq_ref, gk_ref, sl_ref, y_ref,
                    qn_s, kn_s, q4_s, k4_s, v4_s, pv_s, m_s, l_s):
    qn_s[...] = _head_rms_norm(q_ref[...], gq_ref[...]) * (HEAD_DIM ** -0.5 * LOG2E)
    kn_s[...] = _head_rms_norm(k_ref[...], gk_ref[...])
    for c in range(4):
        dst = slice(c * QUARTER, (c + 1) * QUARTER)
        src = _rows(c, QUARTER, 4)
        q4_s[dst, :] = qn_s[src, :]
        k4_s[dst, :] = kn_s[src, :]
        v4_s[dst, :] = v_ref[src, :]

    lane = _lane_iota((1, LANES))
    low = lane < HEAD_DIM
    qi = lax.broadcasted_iota(jnp.int32, (Q_BLOCK, 2 * Q_BLOCK), 0)
    kj = lax.broadcasted_iota(jnp.int32, (Q_BLOCK, 2 * Q_BLOCK), 1)
    dsub = Q_BLOCK + qi - kj
    slopes = sl_ref[...] * LOG2E
    slope_pair = (slopes[:, 0:1], slopes[:, HEAD_DIM:HEAD_DIM + 1])

    for g, (window, dil) in enumerate(A_GROUPS):
        span = window // dil
        nb = SEQ // dil // Q_BLOCK
        base = jnp.where(dsub >= 0, jnp.where(dsub <= span, (-dil * dsub).astype(F32), NEG_INF), NEG_INF)
        biases = (base * slope_pair[0], base * slope_pair[1])
        q_src, k_src, v_src = (qn_s, kn_s, v_ref) if g == 0 else (q4_s, k4_s, v4_s)
        stride = 4 if dil == 16 else 1

        def load(q0, first, stride=stride, q_src=q_src, k_src=k_src, v_src=v_src):
            qr = _rows(q0, Q_BLOCK, stride)
            kr = qr if first else _rows(q0 - stride * Q_BLOCK, 2 * Q_BLOCK, stride)
            return q_src[qr, :], k_src[kr, :].astype(BF16), v_src[kr, :].astype(BF16)

        def compute(loaded, first, biases=biases):
            n = len(loaded)
            lgs = [
                _dot_nt(jnp.where(low if e == 0 else ~low, qb, 0.0).astype(BF16), kk)
                + (biases[e][:, Q_BLOCK:] if first else biases[e])
                for qb, kk, _ in loaded for e in range(2)
            ]
            ms = [jnp.max(lg, axis=-1, keepdims=True) for lg in lgs]
            prs = [jnp.exp2(lg - m) for lg, m in zip(lgs, ms)]
            ls = [jnp.sum(pr, axis=-1, keepdims=True) for pr in prs]
            pvs = [_dot(pr.astype(BF16), loaded[i // 2][2]) for i, pr in enumerate(prs)]
            return [
                (jnp.where(low, pvs[2 * b], pvs[2 * b + 1]), jnp.where(low, ms[2 * b], ms[2 * b + 1]),
                 jnp.where(low, ls[2 * b], ls[2 * b + 1]))
                for b in range(n)
            ]

        def store(q0, res, g=g, stride=stride):
            qr = _rows(q0, Q_BLOCK, stride)
            pv_s[g, qr, :] = res[0]
            m_s[g, qr, :] = res[1]
            l_s[g, qr, :] = res[2]

        def run_blocks(start_of, count, per_iter, first, load=load, compute=compute, store=store):
            def body(t, carry):
                starts = [start_of(t * per_iter + u) for u in range(per_iter)]
                loaded = [load(q0, first) for q0 in starts]
                results = compute(loaded, first)
                for q0, res in zip(starts, results):
                    store(q0, res)
                return carry
            trips = count // per_iter
            if trips == 1:
                body(0, 0)
            else:
                lax.fori_loop(0, trips, body, 0)

        per = nb - 1
        if dil == 1:
            first_start = lambda idx: 0
            rest_start = lambda idx: Q_BLOCK * (1 + idx)
        elif dil == 4:
            first_start = lambda idx: idx * QUARTER
            rest_start = lambda idx, per=per: lax.div(idx, per) * QUARTER + Q_BLOCK * (1 + lax.rem(idx, per))
        else:
            first_start = lambda idx: lax.rem(idx, 4) * QUARTER + lax.div(idx, 4)
            rest_start = None
        run_blocks(first_start, dil, min(dil, FIRST_PER_ITER), True)
        if per:
            run_blocks(rest_start, dil * per, REST_PER_ITER[dil], False)

    chunk = 2 * Q_BLOCK
    for c in range(4):
        for o in range(0, QUARTER, chunk):
            nat = _rows(c + 4 * o, chunk, 4)
            grp = slice(c * QUARTER + o, c * QUARTER + o + chunk)
            m0, m1, m2 = m_s[0, nat, :], m_s[1, grp, :], m_s[2, grp, :]
            mm = jnp.maximum(jnp.maximum(m0, m1), m2)
            w0, w1, w2 = jnp.exp2(m0 - mm), jnp.exp2(m1 - mm), jnp.exp2(m2 - mm)
            num = w0 * pv_s[0, nat, :] + w1 * pv_s[1, grp, :] + w2 * pv_s[2, grp, :]
            den = w0 * l_s[0, nat, :] + w1 * l_s[1, grp, :] + w2 * l_s[2, grp, :]
            y_ref[nat, :] = num / den


def _dilated(h0, gq_tile, gk_tile, slope_tile):
    h3 = h0.reshape(BATCH, SEQ, E_COLS)
    blk = lambda off: pl.BlockSpec((None, SEQ, LANES), lambda b, p: (b, 0, off // LANES + p))
    par = pl.BlockSpec((1, LANES), lambda b, p: (0, p))
    seq_tile = pltpu.VMEM((SEQ, LANES), F32)
    grp_tile = pltpu.VMEM((len(A_GROUPS), SEQ, LANES), F32)
    return pl.pallas_call(
        _dilated_kernel,
        grid=(BATCH, A_WIDTH // LANES),
        in_specs=[blk(E_QA), blk(E_KA), blk(E_VA), par, par, par],
        out_specs=pl.BlockSpec((None, SEQ, LANES), lambda b, p: (b, 0, p)),
        out_shape=jax.ShapeDtypeStruct((BATCH, SEQ, A_WIDTH), F32),
        scratch_shapes=[seq_tile] * 5 + [grp_tile] * 3,
        compiler_params=_params("arbitrary", "arbitrary"),
        name="dilated",
    )(h3, h3, h3, gq_tile, gk_tile, slope_tile).reshape(BATCH * SEQ, A_WIDTH)


def _count_ge(score_ref, s_len, p):
    x = score_ref[:, 0:s_len]
    return jnp.sum(jnp.where(x >= p, 1.0, 0.0), axis=-1, keepdims=True)


def _key_to_float(key):
    bits = key ^ ((key >> 31) & 0x7FFFFFFF)
    return lax.bitcast_convert_type(bits, F32)


def _dsa_kernel(qb_ref, kv_ref, qi_ref, kiw_ref, wq_ref, gq_ref, gk_ref, gi_ref, o_ref,
                ki_t_s, kb_t_s, v_s, score_s, score_t_s, bias_s, *, buckets):
    i = pl.program_id(1)
    lane = _lane_iota((1, LANES))
    kf = float(IDX_TOPK)

    @pl.when(i == 0)
    def _prep():
        kiw = kiw_ref[...]
        ki = jnp.where(lane < IDX_DIM, kiw, 0.0)
        ms = jnp.sum(ki * ki, axis=-1, keepdims=True) * (1.0 / IDX_DIM)
        kin = ki * lax.rsqrt(ms + EPS) * gi_ref[...]
        kv = kv_ref[...]
        kb = jnp.where(lane < HEAD_DIM, kv, 0.0)
        ms = jnp.sum(kb * kb, axis=-1, keepdims=True) * (1.0 / HEAD_DIM)
        kbn = kb * lax.rsqrt(ms + EPS) * gk_ref[...]
        for c0 in range(0, SEQ, LANES):
            ki_t_s[:, c0:c0 + LANES] = kin[c0:c0 + LANES, :].T[0:IDX_DIM, :].astype(BF16)
            kb_t_s[:, c0:c0 + LANES] = kbn[c0:c0 + LANES, :].T[0:HEAD_DIM, :].astype(BF16)
        v_s[...] = _with_ones_lane(jnp.where(lane >= HEAD_DIM, kv, 0.0), lane, 1)

    qn = _head_rms_norm(qb_ref[...], gq_ref[...]) * (HEAD_DIM ** -0.5 * LOG2E)
    q_heads = [qn[:, h * HEAD_DIM:(h + 1) * HEAD_DIM] for h in range(B_HEADS)]
    qraw = qi_ref[...]
    qi_all = jnp.concatenate(
        [qraw[:, h * IDX_DIM:(h + 1) * IDX_DIM] for h in range(IDX_HEADS)], axis=0).astype(BF16)
    wq = wq_ref[...] * (IDX_HEADS ** -0.5 * IDX_DIM ** -0.5)
    t_idx = i * Q_BLOCK + lax.broadcasted_iota(jnp.int32, (Q_BLOCK, 1), 0)

    def body(s_len):
        s_idx = lax.broadcasted_iota(jnp.int32, (1, s_len), 1)
        causal = s_idx <= t_idx

        sh_all = _dot(qi_all, ki_t_s[:, 0:s_len])
        score = jnp.zeros((Q_BLOCK, s_len), F32)
        for h in range(IDX_HEADS):
            sh = sh_all[h * Q_BLOCK:(h + 1) * Q_BLOCK, :]
            score = score + wq[:, IDX_DIM + h:IDX_DIM + h + 1] * jnp.maximum(sh, 0.0)
        masked = jnp.where(causal, score, NEG_INF)
        score_s[:, 0:s_len] = masked
        for c0 in range(0, s_len, LANES):
            score_t_s[c0:c0 + LANES, :] = masked[:, c0:c0 + LANES].T

        def count_t(p_row):
            acc = jnp.zeros((COUNT_SLAB, Q_BLOCK), F32)
            for r0 in range(0, s_len, COUNT_SLAB):
                acc = acc + jnp.where(score_t_s[r0:r0 + COUNT_SLAB, :] >= p_row, 1.0, 0.0)
            return jnp.sum(acc, axis=0, keepdims=True)

        def step(it, tu):
            bit = lax.shift_left(jnp.int32(1), 31 - it)
            cand = tu | bit
            p = _key_to_float(cand ^ INT_MIN)
            return jnp.where(count_t(p) >= kf, cand, tu)

        tu = lax.fori_loop(0, 32, step, jnp.zeros((1, Q_BLOCK), jnp.int32))
        key = tu ^ INT_MIN
        short_row = (i * Q_BLOCK + lane + 1).astype(F32) <= kf
        lo_row = jnp.where(short_row, F32_LOWEST, _key_to_float(key))
        hi_row = _key_to_float(key + 1)
        cnt_lo_row = jnp.where(short_row, kf, count_t(lo_row))
        eye = (lax.broadcasted_iota(jnp.int32, (Q_BLOCK, Q_BLOCK), 0)
               == lax.broadcasted_iota(jnp.int32, (Q_BLOCK, Q_BLOCK), 1))

        def to_col(row):
            return jnp.sum(jnp.where(eye, row, 0.0), axis=-1, keepdims=True)

        lo = to_col(lo_row)
        bias_s[:, 0:s_len] = jnp.where(score_s[:, 0:s_len] >= lo, 0.0, NEG_INF)

        unresolved = jnp.max(jnp.where(cnt_lo_row != kf, 1.0, 0.0)) > 0.0

        @pl.when(unresolved)
        def _ties():
            hi = to_col(hi_row)
            cnt_lo = to_col(cnt_lo_row)

            def active(lo_, hi_, c_):
                mid = lo_ + (hi_ - lo_) * 0.5
                act = jnp.where(c_ != kf, jnp.where(mid > lo_, jnp.where(mid < hi_, 1.0, 0.0), 0.0), 0.0)
                return mid, act

            def cond(c):
                return c[4] > 0

            def wbody(c):
                lo_, hi_, c_, it, _ = c
                mid, act = active(lo_, hi_, c_)
                cnt = _count_ge(score_s, s_len, mid)
                up = jnp.where(cnt >= kf, act, 0.0) > 0.0
                dn = jnp.where(cnt >= kf, 0.0, act) > 0.0
                lo2 = jnp.where(up, mid, lo_)
                c2 = jnp.where(up, cnt, c_)
                hi2 = jnp.where(dn, mid, hi_)
                _, act2 = active(lo2, hi2, c2)
                go = jnp.where(jnp.max(act2) > 0.0, 1, 0) * jnp.where(it < 200, 1, 0)
                return lo2, hi2, c2, it + 1, go.astype(jnp.int32)

            _, act0 = active(lo, hi, cnt_lo)
            go0 = jnp.where(jnp.max(act0) > 0.0, 1, 0).astype(jnp.int32)
            lo_f, _, _, _, _ = lax.while_loop(cond, wbody, (lo, hi, cnt_lo, jnp.int32(0), go0))

            sc = score_s[:, 0:s_len]
            gt = sc > lo_f
            need = kf - jnp.sum(jnp.where(gt, 1.0, 0.0), axis=-1, keepdims=True)
            eq = jnp.where(sc == lo_f, 1.0, 0.0)
            rr = lax.broadcasted_iota(jnp.int32, (LANES, LANES), 0)
            cc = lax.broadcasted_iota(jnp.int32, (LANES, LANES), 1)
            upper = jnp.where(rr < cc, 1.0, 0.0).astype(BF16)
            carry = jnp.zeros((Q_BLOCK, 1), F32)
            for c0 in range(0, s_len, LANES):
                eq_c = eq[:, c0:c0 + LANES]
                prefix = _dot(eq_c.astype(BF16), upper) + carry
                keep = jnp.where(gt[:, c0:c0 + LANES], 1.0, jnp.where(prefix < need, eq_c, 0.0))
                bias_s[:, c0:c0 + LANES] = jnp.where(keep > 0.0, 0.0, NEG_INF)
                carry = carry + jnp.sum(eq_c, axis=-1, keepdims=True)

        key_pos = (s_idx - i * Q_BLOCK).astype(F32)
        low = lane < HEAD_DIM
        for pr_i in range(B_HEADS // 2):
            q2 = jnp.concatenate(q_heads[2 * pr_i:2 * pr_i + 2], axis=0).astype(BF16)
            qk = _dot(q2, kb_t_s[:, 0:s_len])
            slopes = [2.0 ** (-8.0 * (2 * pr_i + e + 1) / B_HEADS) * LOG2E for e in range(2)]
            lgs = [qk[e * Q_BLOCK:(e + 1) * Q_BLOCK, :] + slopes[e] * key_pos + bias_s[:, 0:s_len]
                   for e in range(2)]
            ms = [jnp.max(lg, axis=-1, keepdims=True) for lg in lgs]
            prs = [jnp.exp2(lg - m).astype(BF16) for lg, m in zip(lgs, ms)]
            pv = _dot(jnp.concatenate(prs, axis=0), v_s[0:s_len, :])
            outs = [pv[e * Q_BLOCK:(e + 1) * Q_BLOCK, :] for e in range(2)]
            outs = [o / o[:, 0:1] for o in outs]
            o_ref[:, pr_i * LANES:(pr_i + 1) * LANES] = jnp.where(
                low, pltpu.roll(outs[0], HEAD_DIM, axis=1), outs[1])

    per = (SEQ // Q_BLOCK) // len(buckets)
    for bi, s_len in enumerate(buckets):
        pl.when(i // per == bi)(functools.partial(body, s_len))


def _dsa(h0, gq_tile, gk_pad, gi_pad, buckets):
    h3 = h0.reshape(BATCH, SEQ, E_COLS)
    nq = SEQ // Q_BLOCK
    return pl.pallas_call(
        functools.partial(_dsa_kernel, buckets=buckets),
        grid=(BATCH, nq),
        in_specs=[
            pl.BlockSpec((None, Q_BLOCK, B_WIDTH), lambda b, i: (b, i, E_QB // B_WIDTH)),
            pl.BlockSpec((None, SEQ, LANES), lambda b, i: (b, 0, E_KV // LANES)),
            pl.BlockSpec((None, Q_BLOCK, 2 * LANES), lambda b, i: (b, i, E_QI // (2 * LANES))),
            pl.BlockSpec((None, SEQ, LANES), lambda b, i: (b, 0, E_KIW // LANES)),
            pl.BlockSpec((None, Q_BLOCK, LANES), lambda b, i: (b, i, E_KIW // LANES)),
            pl.BlockSpec((1, B_WIDTH), lambda b, i: (0, 0)),
            pl.BlockSpec((1, LANES), lambda b, i: (0, 0)),
            pl.BlockSpec((1, LANES), lambda b, i: (0, 0)),
        ],
        out_specs=pl.BlockSpec((None, Q_BLOCK, B_WIDTH), lambda b, i: (b, i, 0)),
        out_shape=jax.ShapeDtypeStruct((BATCH, SEQ, B_WIDTH), F32),
        scratch_shapes=[
            pltpu.VMEM((IDX_DIM, SEQ), BF16),
            pltpu.VMEM((HEAD_DIM, SEQ), BF16),
            pltpu.VMEM((SEQ, LANES), BF16),
            pltpu.VMEM((Q_BLOCK, SEQ), F32),
            pltpu.VMEM((SEQ, Q_BLOCK), F32),
            pltpu.VMEM((Q_BLOCK, SEQ), F32),
        ],
        compiler_params=_params("arbitrary", "arbitrary"),
        name="dsa",
    )(h3, h3, h3, h3, h3, gq_tile, gk_pad, gi_pad).reshape(BATCH * SEQ, B_WIDTH)


def _layer_boundary_kernel(ya, za, yb, zb, x, w, g1, w1, x1_ref, h1_ref, *, chunk):
    ga = (ya[...] * _silu(za[...])).astype(BF16)
    gb = (yb[...] * _silu(zb[...])).astype(BF16)
    x1 = x[...] + _dot(ga, w[0:A_WIDTH, :]) + _dot(gb, w[A_WIDTH:, :])
    x1_ref[...] = x1
    ms = jnp.mean(x1 * x1, axis=-1, keepdims=True)
    xn = (x1 * lax.rsqrt(ms + EPS) * g1[...]).astype(BF16)
    n = h1_ref.shape[-1]
    for c in range(0, n, chunk):
        hi = min(c + chunk, n)
        h1_ref[:, c:hi] = _dot(xn, w1[:, c:hi])


def _layer_boundary(ya, h0, yb, x2d, w_bf16, g1, w1_bf16, chunk):
    m = x2d.shape[0]
    n1 = w1_bf16.shape[1]
    rows = FUSED_ROW_TILE
    half = pl.BlockSpec((rows, A_WIDTH), lambda i: (i, 0))
    once = pl.Buffered(1)
    return pl.pallas_call(
        functools.partial(_layer_boundary_kernel, chunk=chunk),
        grid=(m // rows,),
        in_specs=[
            half,
            pl.BlockSpec((rows, A_WIDTH), lambda i: (i, E_ZA // A_WIDTH)),
            half,
            pl.BlockSpec((rows, B_WIDTH), lambda i: (i, E_ZB // B_WIDTH)),
            pl.BlockSpec((rows, D_MODEL), lambda i: (i, 0)),
            pl.BlockSpec((A_WIDTH + B_WIDTH, D_MODEL), lambda i: (0, 0), pipeline_mode=once),
            pl.BlockSpec((1, D_MODEL), lambda i: (0, 0)),
            pl.BlockSpec((D_MODEL, n1), lambda i: (0, 0), pipeline_mode=once),
        ],
        out_specs=[
            pl.BlockSpec((rows, D_MODEL), lambda i: (i, 0)),
            pl.BlockSpec((rows, n1), lambda i: (i, 0)),
        ],
        out_shape=[
            jax.ShapeDtypeStruct((m, D_MODEL), F32),
            jax.ShapeDtypeStruct((m, n1), F32),
        ],
        compiler_params=_params("arbitrary"),
        name="layer_boundary",
    )(ya, h0, yb, h0, x2d, w_bf16, g1, w1_bf16)


def _forget_kernel(fg_ref, b_ref, crow_ref):
    x = fg_ref[...] + b_ref[...]
    logf = jnp.minimum(x, 0.0) - jnp.log1p(jnp.exp(-jnp.abs(x)))
    rr = lax.broadcasted_iota(jnp.int32, (LANES, LANES), 0)
    cc = lax.broadcasted_iota(jnp.int32, (LANES, LANES), 1)
    tri = jnp.where(cc <= rr, 1.0, 0.0).astype(BF16)
    carry = jnp.zeros((1, LANES), F32)
    for blk in range(SEQ // LANES):
        rows = slice(blk * LANES, (blk + 1) * LANES)
        v = logf[rows, :]
        t0 = v.astype(BF16)
        r1 = v - t0.astype(F32)
        t1 = r1.astype(BF16)
        t2 = (r1 - t1.astype(F32)).astype(BF16)
        c = _dot(tri, t0) + _dot(tri, t1) + _dot(tri, t2) + carry
        crow_ref[:, rows] = c.T[0:C_HEADS, :]
        carry = c[LANES - 1:LANES, :]


def _forget_cumsum(h1, b_pad):
    h3 = h1.reshape(BATCH, SEQ, O_COLS)
    return pl.pallas_call(
        _forget_kernel,
        grid=(BATCH,),
        in_specs=[
            pl.BlockSpec((None, SEQ, LANES), lambda b: (b, 0, O_FG // LANES)),
            pl.BlockSpec((1, LANES), lambda b: (0, 0)),
        ],
        out_specs=pl.BlockSpec((None, C_HEADS, SEQ), lambda b: (b, 0, 0)),
        out_shape=jax.ShapeDtypeStruct((BATCH, C_HEADS, SEQ), F32),
        compiler_params=_params("arbitrary"),
        name="forget_cumsum",
    )(h3, b_pad)


def _fox_kernel(q_ref, k_ref, v_ref, crow_ref, gq_ref, gk_ref, y_ref, qn_s, kp_t_s, vp_s):
    lane = _lane_iota((1, LANES))
    low = lane < HEAD_DIM
    qn_s[...] = (_head_rms_norm(q_ref[...], gq_ref[...]) * (HEAD_DIM ** -0.5 * LOG2E)).astype(BF16)
    kn = _head_rms_norm(k_ref[...], gk_ref[...])
    v = v_ref[...]
    for c0 in range(0, SEQ, LANES):
        kc = kn[c0:c0 + LANES, :]
        kp_t_s[0, :, c0:c0 + LANES] = jnp.where(low, kc, 0.0).T.astype(BF16)
        kp_t_s[1, :, c0:c0 + LANES] = jnp.where(low, 0.0, kc).T.astype(BF16)
    vp_s[0] = _with_ones_lane(jnp.where(low, v, 0.0), lane, 0)
    vp_s[1] = _with_ones_lane(jnp.where(low, 0.0, v), lane, 1)

    crow = crow_ref[...]
    on_or_below = (lax.broadcasted_iota(jnp.int32, (FOX_Q_BLOCK, FOX_Q_BLOCK), 1)
                   <= lax.broadcasted_iota(jnp.int32, (FOX_Q_BLOCK, FOX_Q_BLOCK), 0))
    for iq in range(SEQ // FOX_Q_BLOCK):
        rows = slice(iq * FOX_Q_BLOCK, (iq + 1) * FOX_Q_BLOCK)
        s_len = (iq + 1) * FOX_Q_BLOCK
        d0 = s_len - FOX_Q_BLOCK
        qblk = qn_s[rows, :]
        lgs = []
        for e in range(2):
            decay = (crow[e:e + 1, d0:d0 + 1] - crow[e:e + 1, 0:s_len]) * LOG2E
            lg = _dot(qblk, kp_t_s[e, :, 0:s_len]) + decay
            tail = jnp.where(on_or_below, lg[:, d0:], NEG_INF)
            lgs.append(tail if d0 == 0 else jnp.concatenate([lg[:, :d0], tail], axis=-1))
        ms = [jnp.max(lg, axis=-1, keepdims=True) for lg in lgs]
        prs = [jnp.exp2(lg - m) for lg, m in zip(lgs, ms)]
        y_ref[rows, :] = _pair_output(prs, vp_s, s_len, low)


def _fox(h1, crow, gq_tile, gk_tile):
    h3 = h1.reshape(BATCH, SEQ, O_COLS)
    crow4 = crow.reshape(BATCH, C_HEADS // 2, 2, SEQ)
    blk = lambda off: pl.BlockSpec((None, SEQ, LANES), lambda b, p: (b, 0, off // LANES + p))
    par = pl.BlockSpec((1, LANES), lambda b, p: (0, 0))
    return pl.pallas_call(
        _fox_kernel,
        grid=(BATCH, C_HEADS // 2),
        in_specs=[
            blk(O_Q), blk(O_K), blk(O_V),
            pl.BlockSpec((None, None, 2, SEQ), lambda b, p: (b, p, 0, 0)),
            par, par,
        ],
        out_specs=pl.BlockSpec((None, SEQ, LANES), lambda b, p: (b, 0, p)),
        out_shape=jax.ShapeDtypeStruct((BATCH, SEQ, C_WIDTH), F32),
        scratch_shapes=[
            pltpu.VMEM((SEQ, LANES), BF16),
            pltpu.VMEM((2, LANES, SEQ), BF16),
            pltpu.VMEM((2, SEQ, LANES), BF16),
        ],
        compiler_params=_params("arbitrary", "arbitrary"),
        name="fox",
    )(h3, h3, h3, crow4, gq_tile, gk_tile).reshape(BATCH * SEQ, C_WIDTH)


def _outproj_odd_kernel(y, z, x, w, out):
    g = (y[...] * _silu(z[...])).astype(BF16)
    out[...] = x[...] + _dot(g, w[...])


def _outproj_odd(y, h1, x2d, w_bf16):
    m = x2d.shape[0]
    full = pl.BlockSpec((ROW_TILE, D_MODEL), lambda i: (i, 0))
    return pl.pallas_call(
        _outproj_odd_kernel,
        grid=(m // ROW_TILE,),
        in_specs=[
            full,
            pl.BlockSpec((ROW_TILE, C_WIDTH), lambda i: (i, O_Z // C_WIDTH)),
            full,
            pl.BlockSpec((C_WIDTH, D_MODEL), lambda i: (0, 0)),
        ],
        out_specs=full,
        out_shape=jax.ShapeDtypeStruct((m, D_MODEL), F32),
        compiler_params=_params("arbitrary"),
        name="outproj_odd",
    )(y, h1, x2d, w_bf16)


def _even_weight(w):
    qa, ka, va, za, qb, kb, vb, zb, qi, ki, wi = jnp.split(
        w, np.cumsum([512, 512, 512, 512, 512, 64, 64, 512, 256, 32, 8])[:-1].tolist(), axis=-1)
    pad = jnp.zeros((w.shape[0], E_COLS - (E_KIW + IDX_DIM + IDX_HEADS)), w.dtype)
    return jnp.concatenate([qa, ka, va, za, qb, zb, qi, kb, vb, ki, wi, pad], axis=-1).astype(BF16)


def _pad_lanes(v, width):
    return jnp.pad(v, (0, width - v.shape[0])).reshape(1, width)


def _tile_heads(g, heads):
    return jnp.tile(g, heads).reshape(1, heads * HEAD_DIM)


def kernel(x, even_norm, even_w_in, even_q_norm_a, even_k_norm_a, even_q_norm_b, even_k_norm_b,
           even_k_norm_idx, even_w_out, odd_norm, odd_w_in, odd_b_forget, odd_q_norm, odd_k_norm,
           odd_w_out):
    assert x.shape == (BATCH, SEQ, D_MODEL)
    x2d = x.reshape(BATCH * SEQ, D_MODEL)

    h0 = _inproj(x2d, even_norm[0].reshape(1, D_MODEL), _even_weight(even_w_in[0]), chunk=512)
    slope_tile = jnp.asarray(
        np.repeat([2.0 ** (-8.0 * (i + 1) / A_HEADS) for i in range(A_HEADS)], HEAD_DIM), F32
    ).reshape(1, A_WIDTH)
    gqa = _tile_heads(even_q_norm_a[0], A_HEADS)
    gka = _tile_heads(even_k_norm_a[0], A_HEADS)
    ya = _dilated(h0, gqa, gka, slope_tile)
    yb = _dsa(
        h0,
        _tile_heads(even_q_norm_b[0], B_HEADS),
        _pad_lanes(even_k_norm_b[0], LANES),
        _pad_lanes(even_k_norm_idx[0], LANES),
        buckets=tuple(range(256, SEQ + 1, 256)),
    )
    w1 = jnp.pad(odd_w_in[0], ((0, 0), (0, O_COLS - odd_w_in.shape[-1]))).astype(BF16)
    x1, h1 = _layer_boundary(ya, h0, yb, x2d, even_w_out[0].astype(BF16),
                             odd_norm[0].reshape(1, D_MODEL), w1, chunk=512)
    crow = _forget_cumsum(h1, _pad_lanes(odd_b_forget[0], LANES))
    y = _fox(h1, crow, _tile_heads(odd_q_norm[0], 2), _tile_heads(odd_k_norm[0], 2))
    out = _outproj_odd(y, h1, x1, odd_w_out[0].astype(BF16))
    return out.reshape(BATCH, SEQ, D_MODEL)
```
